```python
import math
import jax, jax.numpy as jnp
from jax import lax
import numpy as np

D_MODEL = 1024
BATCH = 32
SEQ = 256
DEPTH = 1
DEC_BATCH = 4
DEC_SEQ = 1024
PAST_LEN = 256

GRID_W = 64
N_HEADS_ATTN = 8
HEAD_DIM = 64
ATTN_WIDTH = N_HEADS_ATTN * HEAD_DIM
HY_CH = D_MODEL - ATTN_WIDTH
MIX_WIDTH = ATTN_WIDTH + HY_CH
IN_COLS = 3 * ATTN_WIDTH + 3 * HY_CH
WIN_ROWS = 8
WIN_COLS = 16
D_FF = 2816
HY_ORDER = 2
FILT_FREQS = 8
FILT_EMB = 2 * FILT_FREQS + 1
FILT_HID = 64
DECAY_TARGET = 1e-2
FAST_DECAY_PCT = 0.3
SLOW_DECAY_PCT = 1.5
MAX_DECAY = math.log(DECAY_TARGET) / FAST_DECAY_PCT
MIN_DECAY = math.log(DECAY_TARGET) / SLOW_DECAY_PCT
Q_BLOCK = 128
EPS = 1e-6
NEG = -1e30

kernel_name = "hymba_natten_hyena_prefix_dit_step"


def _rmsnorm(x, g):
    xf = x.astype(jnp.float32)
    y = xf * lax.rsqrt(jnp.mean(xf * xf, axis=-1, keepdims=True) + EPS)
    return (y * g.astype(jnp.float32)).astype(x.dtype)


def _adaln(cvec, w_ada, b_ada):
    m = jax.nn.silu(cvec) @ w_ada + b_ada
    return [t[:, None, :] for t in jnp.split(m, 6, axis=-1)]


def _dwconv3(x, w, b):
    xp = jnp.pad(x, ((0, 0), (1, 1), (0, 0)))
    return xp[:, :-2] * w[0] + xp[:, 1:-1] * w[1] + xp[:, 2:] * w[2] + b


def _filter_spectrum(L, w1, b1, w2, b2, w3, b3, freq):
    t = jnp.arange(L, dtype=jnp.float32) / L
    f = jnp.arange(1, FILT_FREQS + 1, dtype=jnp.float32)
    ang = 2.0 * math.pi * t[:, None] * f[None, :]
    z = jnp.concatenate([t[:, None], jnp.cos(ang), jnp.sin(ang)], axis=-1)
    h = jnp.sin(freq * (z @ w1 + b1))
    h = jnp.sin(freq * (h @ w2 + b2))
    h = (h @ w3 + b3).astype(jnp.float32).reshape(L, HY_ORDER, 2, HY_CH)
    deltas = jnp.abs(jnp.linspace(MIN_DECAY, MAX_DECAY, HY_CH, dtype=jnp.float32))
    decay = jnp.exp(-t[:, None] * deltas[None, :])
    h = h * decay[:, None, None, :]
    fwd, bwd = h[:, :, 0], h[:, :, 1]
    k_full = jnp.concatenate([fwd, jnp.zeros((1, HY_ORDER, HY_CH), jnp.float32),
                              jnp.flip(bwd[1:], axis=0)], axis=0)
    return jnp.fft.rfft(k_full, axis=0)


def _long_conv(u, k_f, bias):
    L = u.shape[1]
    uf = u.astype(jnp.float32)
    y = jnp.fft.irfft(jnp.fft.rfft(uf, n=2 * L, axis=1) * k_f[None], n=2 * L, axis=1)[:, :L]
    return (y + uf * bias.astype(jnp.float32)).astype(u.dtype)


def _hyena(hy, k_f, filt_bias):
    x1, x2, v = jnp.split(hy, 3, axis=-1)
    z = x1 * _long_conv(v, k_f[:, 0], filt_bias[0])
    return x2 * _long_conv(z, k_f[:, 1], filt_bias[1])


def _ctx_attention(q, k, v):
    B, H, L, dh = q.shape
    nb = L // Q_BLOCK
    qb = q.reshape(B, H, nb, Q_BLOCK, dh).transpose(2, 0, 1, 3, 4)
    scale = HEAD_DIM ** -0.5

    def blk(qi):
        s = jnp.einsum('bhqd,bhkd->bhqk', qi, k).astype(jnp.float32) * scale
        p = jax.nn.softmax(s, axis=-1).astype(v.dtype)
        return jnp.einsum('bhqk,bhkd->bhqd', p, v)

    o = lax.map(blk, qb)
    return o.transpose(1, 2, 0, 3, 4).reshape(B, H, L, dh)


def _neighbourhood_attention(q, k, v, k_ctx, v_ctx, rpb):
    B, H, N, dh = q.shape
    R = N // GRID_W
    W = GRID_W
    wr = min(WIN_ROWS, R)
    scale = HEAD_DIM ** -0.5
    qg = q.reshape(B, H, R, W, dh)
    kg = k.reshape(B, H, R, W, dh)
    vg = v.reshape(B, H, R, W, dh)
    r = jnp.arange(R)
    row_start = jnp.clip(r - wr // 2, 0, R - wr)
    row_idx = row_start[:, None] + jnp.arange(wr)[None, :]
    k_band = kg[:, :, row_idx]
    v_band = vg[:, :, row_idx]
    col = jnp.arange(W)
    col_start = jnp.clip(col - WIN_COLS // 2, 0, W - WIN_COLS)
    col_in = (col[None, :] >= col_start[:, None]) & (col[None, :] < col_start[:, None] + WIN_COLS)
    dr = row_idx - r[:, None] + (WIN_ROWS - 1)
    dc = jnp.clip(col[None, :] - col[:, None], -(WIN_COLS - 1), WIN_COLS - 1) + (WIN_COLS - 1)
    bias = rpb[:, dr[:, None, :, None], dc[None, :, None, :]].astype(jnp.float32)
    s_loc = jnp.einsum('bhrqd,bhrikd->bhrqik', qg, k_band).astype(jnp.float32) * scale
    s_loc = jnp.where(col_in[None, None, None, :, None, :], s_loc + bias[None], NEG)
    s_loc = s_loc.reshape(B, H, R, W, wr * W)
    s_ctx = jnp.einsum('bhrqd,bhkd->bhrqk', qg, k_ctx).astype(jnp.float32) * scale
    p = jax.nn.softmax(jnp.concatenate([s_loc, s_ctx], axis=-1), axis=-1).astype(v.dtype)
    p_loc = p[..., :wr * W].reshape(B, H, R, W, wr, W)
    p_ctx = p[..., wr * W:]
    o = (jnp.einsum('bhrqik,bhrikd->bhrqd', p_loc, v_band)
         + jnp.einsum('bhrqk,bhkd->bhrqd', p_ctx, v_ctx))
    return o.reshape(B, H, N, dh)


def _heads(t):
    B, L, _ = t.shape
    return t.reshape(B, L, N_HEADS_ATTN, HEAD_DIM).transpose(0, 2, 1, 3)


def _layer(x, mods, attend, norm1_g, w_in, hy_conv_w, hy_conv_b, k_f, filt_bias,
           grp_norm_g, w_out, norm2_g, w_up, ffn_conv_w, ffn_conv_b, w_down):
    sh1, sc1, g1, sh2, sc2, g2 = mods
    B, L, _ = x.shape
    h = _rmsnorm(x, norm1_g) * (1.0 + sc1) + sh1
    proj = h @ w_in
    q = _heads(proj[..., :ATTN_WIDTH])
    k = _heads(proj[..., ATTN_WIDTH:2 * ATTN_WIDTH])
    v = _heads(proj[..., 2 * ATTN_WIDTH:3 * ATTN_WIDTH])
    hy = _dwconv3(proj[..., 3 * ATTN_WIDTH:], hy_conv_w, hy_conv_b)
    a = attend(q, k, v).transpose(0, 2, 1, 3).reshape(B, L, ATTN_WIDTH)
    hyo = _hyena(hy, k_f, filt_bias)
    merged = jnp.concatenate([_rmsnorm(a, grp_norm_g[:ATTN_WIDTH]),
                              _rmsnorm(hyo, grp_norm_g[ATTN_WIDTH:])], axis=-1)
    x = x + g1 * (merged @ w_out)
    h2 = _rmsnorm(x, norm2_g) * (1.0 + sc2) + sh2
    u = _dwconv3(h2 @ w_up, ffn_conv_w, ffn_conv_b)
    gate, up = jnp.split(u, 2, axis=-1)
    x = x + g2 * ((jax.nn.silu(gate) * up) @ w_down)
    return x, k, v


def setup_inputs(seed: int = 0) -> dict:
    key = jax.random.key(seed)
    ks = jax.random.split(key, 32)
    n = jax.random.normal
    f32 = jnp.float32
    D = D_MODEL
    return {
        "x_prompt": n(ks[0], (BATCH, SEQ, D), f32),
        "x_sample": n(ks[1], (DEC_BATCH, DEC_SEQ, D), f32),
        "cache_ctx_k": n(ks[2], (DEC_BATCH, DEPTH, N_HEADS_ATTN, PAST_LEN, HEAD_DIM), f32),
        "cache_ctx_v": n(ks[3], (DEC_BATCH, DEPTH, N_HEADS_ATTN, PAST_LEN, HEAD_DIM), f32),
        "c": n(ks[4], (DEC_BATCH, D), f32),
        "c_ctx": n(ks[5], (D,), f32),
        "w_ada": n(ks[6], (DEPTH, D, 6 * D), f32) * (0.5 * D ** -0.5),
        "b_ada": n(ks[7], (DEPTH, 6 * D), f32) * 0.02,
        "norm1_g": 1.0 + 0.05 * n(ks[8], (DEPTH, D), f32),
        "w_in": n(ks[9], (DEPTH, D, IN_COLS), f32) * D ** -0.5,
        "rpb": n(ks[10], (DEPTH, N_HEADS_ATTN, 2 * WIN_ROWS - 1, 2 * WIN_COLS - 1), f32) * 0.1,
        "hy_conv_w": n(ks[11], (DEPTH, 3, 3 * HY_CH), f32) * (3.0 ** -0.5),
        "hy_conv_b": n(ks[12], (DEPTH, 3 * HY_CH), f32) * 0.02,
        "filt_w1": n(ks[13], (DEPTH, FILT_EMB, FILT_HID), f32) * FILT_EMB ** -0.5,
        "filt_b1": n(ks[14], (DEPTH, FILT_HID), f32) * 0.1,
        "filt_w2": n(ks[15], (DEPTH, FILT_HID, FILT_HID), f32) * FILT_HID ** -0.5,
        "filt_b2": n(ks[16], (DEPTH, FILT_HID), f32) * 0.1,
        "filt_w3": n(ks[17], (DEPTH, FILT_HID, HY_ORDER * 2 * HY_CH), f32) * (0.05 * FILT_HID ** -0.5),
        "filt_b3": n(ks[18], (DEPTH, HY_ORDER * 2 * HY_CH), f32) * 0.01,
        "filt_freq": 1.0 + 0.1 * n(ks[19], (DEPTH, FILT_HID), f32),
        "filt_bias": n(ks[20], (DEPTH, HY_ORDER, HY_CH), f32) * 0.1,
        "grp_norm_g": 1.0 + 0.05 * n(ks[21], (DEPTH, MIX_WIDTH), f32),
        "w_out": n(ks[22], (DEPTH, MIX_WIDTH, D), f32) * MIX_WIDTH ** -0.5,
        "norm2_g": 1.0 + 0.05 * n(ks[23], (DEPTH, D), f32),
        "w_up": n(ks[24], (DEPTH, D, 2 * D_FF), f32) * D ** -0.5,
        "ffn_conv_w": n(ks[25], (DEPTH, 3, 2 * D_FF), f32) * (3.0 ** -0.5),
        "ffn_conv_b": n(ks[26], (DEPTH, 2 * D_FF), f32) * 0.02,
        "w_down": n(ks[27], (DEPTH, D_FF, D), f32) * D_FF ** -0.5,
        "final_g": 1.0 + 0.05 * n(ks[28], (D,), f32),
    }


def reference(x_prompt, x_sample, cache_ctx_k, cache_ctx_v, c, c_ctx, w_ada, b_ada, norm1_g,
              w_in, rpb, hy_conv_w, hy_conv_b, filt_w1, filt_b1, filt_w2, filt_b2, filt_w3,
              filt_b3, filt_freq, filt_bias, grp_norm_g, w_out, norm2_g, w_up, ffn_conv_w,
              ffn_conv_b, w_down, final_g):
    xc = x_prompt
    xs = x_sample
    L_ctx = xc.shape[1]
    L_lat = xs.shape[1]
    ks_out, vs_out = [], []
    for l in range(DEPTH):
        shared = (norm1_g[l], w_in[l], hy_conv_w[l], hy_conv_b[l])
        tail = (grp_norm_g[l], w_out[l], norm2_g[l], w_up[l], ffn_conv_w[l], ffn_conv_b[l], w_down[l])
        filt = (filt_w1[l], filt_b1[l], filt_w2[l], filt_b2[l], filt_w3[l], filt_b3[l], filt_freq[l])
        mods_ctx = _adaln(c_ctx[None, :], w_ada[l], b_ada[l])
        kf_ctx = _filter_spectrum(L_ctx, *filt)
        xc, k_c, v_c = _layer(xc, mods_ctx, _ctx_attention, *shared, kf_ctx, filt_bias[l], *tail)
        ks_out.append(k_c)
        vs_out.append(v_c)
        mods_lat = _adaln(c, w_ada[l], b_ada[l])
        kf_lat = _filter_spectrum(L_lat, *filt)
        k_cache = cache_ctx_k[:, l]
        v_cache = cache_ctx_v[:, l]
        rpb_l = rpb[l]
        attend_lat = lambda q, k, v: _neighbourhood_attention(q, k, v, k_cache, v_cache, rpb_l)
        xs, _, _ = _layer(xs, mods_lat, attend_lat, *shared, kf_lat, filt_bias[l], *tail)
    y_prompt = _rmsnorm(xc, final_g)
    y_sample = _rmsnorm(xs, final_g)
    state_ctx_k = jnp.stack(ks_out, axis=1)
    state_ctx_v = jnp.stack(vs_out, axis=1)
    return (y_prompt, y_sample, state_ctx_k, state_ctx_v)
```

```python
import functools
import math

import numpy as np
import jax
import jax.numpy as jnp
from jax import lax
from jax.experimental import pallas as pl
from jax.experimental.pallas import tpu as pltpu

F32 = jnp.float32
BF16 = jnp.bfloat16
HIGHEST = lax.Precision.HIGHEST

D_MODEL = 1024
N_HEADS = 8
HEAD_DIM = 64
ATTN_W = N_HEADS * HEAD_DIM
HY_CH = D_MODEL - ATTN_W
IN_COLS = 3 * ATTN_W + 3 * HY_CH
D_FF = 2816
GRID_W = 64
WIN_ROWS = 8
WIN_COLS = 16
FILT_FREQS = 8
FILT_HID = 64
FILT_EMB_PAD = 32
DECAY_TARGET = 1e-2
MAX_DECAY = math.log(DECAY_TARGET) / 0.3
MIN_DECAY = math.log(DECAY_TARGET) / 1.5
EPS = 1e-6
NEG = -1e30

LANES = 128
ROW_BLOCK = 1024
PROJ_ROWS = 512
FF_TILE = 256
N_FF_TILES = D_FF // FF_TILE
Q_ROWS = 2
KEY_ROWS = 10
VMEM_LIMIT = 56 * 1024 * 1024


def _cparams(sem):
    return pltpu.CompilerParams(dimension_semantics=sem, vmem_limit_bytes=VMEM_LIMIT)


def _rms(x):
    return x * lax.rsqrt(jnp.mean(x * x, axis=-1, keepdims=True) + EPS)


def _dot(a, b):
    return jnp.dot(a, b, preferred_element_type=F32)


def _dot_t(a, b):
    return lax.dot_general(a, b, (((1,), (1,)), ((), ())), preferred_element_type=F32)


def _mods_kernel(c_ref, w_ref, b_ref, o_ref):
    cv = c_ref[...]
    s = cv / (1.0 + jnp.exp(-cv))
    o_ref[0] = jnp.dot(s, w_ref[...], precision=HIGHEST, preferred_element_type=F32) + b_ref[0]


def _mods(cvec, w_ada, b_ada):
    return pl.pallas_call(
        _mods_kernel,
        grid=(6,),
        in_specs=[
            pl.BlockSpec((8, D_MODEL), lambda j: (0, 0)),
            pl.BlockSpec((D_MODEL, D_MODEL), lambda j: (0, j)),
            pl.BlockSpec((1, 1, D_MODEL), lambda j: (j, 0, 0)),
        ],
        out_specs=pl.BlockSpec((1, 8, D_MODEL), lambda j: (j, 0, 0)),
        out_shape=jax.ShapeDtypeStruct((6, 8, D_MODEL), F32),
        compiler_params=_cparams(("arbitrary",)),
        name="mods",
    )(cvec, w_ada, b_ada.reshape(6, 1, D_MODEL))


def _dft_tables(L):
    f = np.arange(L, dtype=np.int64)[:, None]
    t = np.arange(L, dtype=np.int64)[None, :]
    ang = np.pi * ((f * t) % (2 * L)).astype(np.float64) / L
    top = np.cos(ang)
    bot = -np.sin(ang)
    bot[0, :] = np.where(np.arange(L) % 2 == 0, 1.0, -1.0)
    fwd = np.concatenate([top, bot], axis=0)
    w = np.full((2 * L,), 1.0 / L)
    w[0] = w[L] = 0.5 / L
    inv = (fwd * w[:, None]).T
    return fwd.astype(np.float32), inv.astype(np.float32)


def _filter_consts(L):
    t = np.arange(L, dtype=np.float64) / L
    fr = np.arange(1, FILT_FREQS + 1, dtype=np.float64)
    ang = 2.0 * math.pi * t[:, None] * fr[None, :]
    z = np.zeros((L, FILT_EMB_PAD), np.float64)
    z[:, 0] = t
    z[:, 1:1 + FILT_FREQS] = np.cos(ang)
    z[:, 1 + FILT_FREQS:1 + 2 * FILT_FREQS] = np.sin(ang)
    deltas = np.abs(np.linspace(MIN_DECAY, MAX_DECAY, HY_CH))
    decay = np.exp(-t[:, None] * deltas[None, :])
    return z.astype(np.float32), decay.astype(np.float32)


def _filter_kernel(z_ref, w1_ref, b1_ref, w2_ref, b2_ref, w3_ref, b3_ref, fr_ref, dec_ref,
                   a_ref, o_ref, *, L):
    fr = fr_ref[...]
    h = jnp.sin(fr * (jnp.dot(z_ref[...], w1_ref[...], precision=HIGHEST,
                              preferred_element_type=F32) + b1_ref[...]))
    h = jnp.sin(fr * (jnp.dot(h, w2_ref[...], precision=HIGHEST,
                              preferred_element_type=F32) + b2_ref[...]))
    h = jnp.dot(h, w3_ref[...], precision=HIGHEST, preferred_element_type=F32) + b3_ref[...]
    dec = dec_ref[...]
    row0 = lax.broadcasted_iota(jnp.int32, (L, 1), 0) == 0
    fwd = h[:, :HY_CH] * dec
    bwd = jnp.where(row0, 0.0, h[:, HY_CH:] * dec)
    gp = (fwd + bwd).astype(BF16)
    gm = (fwd - bwd).astype(BF16)
    top = _dot(a_ref[0:L, :], gp)
    bot = _dot(a_ref[L:2 * L, :], gm)
    nyq = _dot(a_ref[L:L + 16, :], gp)[0:1]
    o_ref[0, 0:L, :] = top
    o_ref[0, L:2 * L, :] = jnp.where(row0, nyq, bot)


def _filter_spectrum(L, a_fwd, w1, b1, w2, b2, w3, b3, freq):
    z, decay = _filter_consts(L)
    full = lambda shape: pl.BlockSpec(shape, lambda o: (0,) * len(shape))
    return pl.pallas_call(
        functools.partial(_filter_kernel, L=L),
        grid=(2,),
        in_specs=[
            full((L, FILT_EMB_PAD)),
            full((FILT_EMB_PAD, FILT_HID)),
            full((1, FILT_HID)),
            full((FILT_HID, FILT_HID)),
            full((1, FILT_HID)),
            pl.BlockSpec((FILT_HID, 2 * HY_CH), lambda o: (0, o)),
            pl.BlockSpec((1, 2 * HY_CH), lambda o: (0, o)),
            full((1, FILT_HID)),
            full((L, HY_CH)),
            full((2 * L, L)),
        ],
        out_specs=pl.BlockSpec((1, 2 * L, HY_CH), lambda o: (o, 0, 0)),
        out_shape=jax.ShapeDtypeStruct((2, 2 * L, HY_CH), F32),
        compiler_params=_cparams(("arbitrary",)),
        name=f"filter_{L}",
    )(jnp.asarray(z), jnp.pad(w1, ((0, FILT_EMB_PAD - w1.shape[0]), (0, 0))), b1[None], w2, b2[None],
      w3, b3[None], freq[None], jnp.asarray(decay), a_fwd)


def _proj_kernel(x_ref, mods_ref, g_ref, w_ref, qkv_ref, hy_ref, *kv_ref, mod_base, blocks_per_mod):
    r = mod_base + pl.program_id(0) // blocks_per_mod
    sh1 = mods_ref[0, pl.ds(r, 1), :]
    sc1 = mods_ref[1, pl.ds(r, 1), :]
    h = (_rms(x_ref[...]) * g_ref[...] * (1.0 + sc1) + sh1).astype(BF16)
    q = _dot(h, w_ref[:, 0:ATTN_W])
    qkv_ref[:, 0:ATTN_W] = (q * HEAD_DIM ** -0.5).astype(BF16)
    k = _dot(h, w_ref[:, ATTN_W:2 * ATTN_W])
    qkv_ref[:, ATTN_W:2 * ATTN_W] = k.astype(BF16)
    v = _dot(h, w_ref[:, 2 * ATTN_W:3 * ATTN_W])
    qkv_ref[:, 2 * ATTN_W:3 * ATTN_W] = v.astype(BF16)
    if kv_ref:
        kv_ref[0][:, 0:ATTN_W] = k
        kv_ref[0][:, ATTN_W:2 * ATTN_W] = v
    hy_ref[...] = _dot(h, w_ref[:, 3 * ATTN_W:])


def _proj(x, mods, g, w_in, *, mod_base, blocks_per_mod, emit_kv):
    n = x.shape[0]
    row = lambda w: pl.BlockSpec((PROJ_ROWS, w), lambda i: (i, 0))
    out_specs = [row(3 * ATTN_W), row(3 * HY_CH)]
    out_shape = [jax.ShapeDtypeStruct((n, 3 * ATTN_W), BF16), jax.ShapeDtypeStruct((n, 3 * HY_CH), F32)]
    if emit_kv:
        out_specs.append(row(2 * ATTN_W))
        out_shape.append(jax.ShapeDtypeStruct((n, 2 * ATTN_W), F32))
    return pl.pallas_call(
        functools.partial(_proj_kernel, mod_base=mod_base, blocks_per_mod=blocks_per_mod),
        grid=(n // PROJ_ROWS,),
        in_specs=[
            row(D_MODEL),
            pl.BlockSpec((6, 8, D_MODEL), lambda i: (0, 0, 0)),
            pl.BlockSpec((1, D_MODEL), lambda i: (0, 0)),
            pl.BlockSpec((D_MODEL, IN_COLS), lambda i: (0, 0)),
        ],
        out_specs=out_specs,
        out_shape=out_shape,
        compiler_params=_cparams(("arbitrary",)),
        name="proj_ctx" if emit_kv else "proj_lat",
    )(x, mods, g, w_in)


def _softmax_pv(scores, values):
    mx = functools.reduce(jnp.maximum, [jnp.max(s, axis=-1, keepdims=True) for s in scores])
    ps = [jnp.exp(s - mx) for s in scores]
    den = functools.reduce(jnp.add, [jnp.sum(p, axis=-1, keepdims=True) for p in ps])
    num = functools.reduce(jnp.add, [_dot(p.astype(BF16), v) for p, v in zip(ps, values)])
    return num * (1.0 / den)


def _ctx_attn_kernel(qkv_ref, o_ref, *, n_seq, L):
    low = lax.broadcasted_iota(jnp.int32, (1, LANES), 1) < HEAD_DIM

    def seq_body(s, carry):
        rows = pl.ds(pl.multiple_of(s * L, L), L)
        for p in range(ATTN_W // LANES):
            cols = lambda part: slice(part * ATTN_W + p * LANES, part * ATTN_W + (p + 1) * LANES)
            qb = qkv_ref[rows, cols(0)]
            kb = qkv_ref[rows, cols(1)]
            vb = qkv_ref[rows, cols(2)]
            outs = []
            for head_mask in (low, jnp.logical_not(low)):
                ke = jnp.where(head_mask, kb, jnp.zeros_like(kb))
                outs.append(_softmax_pv([_dot_t(qb, ke)], [vb]))
            o_ref[rows, p * LANES:(p + 1) * LANES] = jnp.where(low, outs[0], outs[1])
        return carry

    lax.fori_loop(0, n_seq, seq_body, 0)


def _ctx_attn(qkv, L):
    n = qkv.shape[0]
    return pl.pallas_call(
        functools.partial(_ctx_attn_kernel, n_seq=ROW_BLOCK // L, L=L),
        grid=(n // ROW_BLOCK,),
        in_specs=[pl.BlockSpec((ROW_BLOCK, 3 * ATTN_W), lambda i: (i, 0))],
        out_specs=pl.BlockSpec((ROW_BLOCK, ATTN_W), lambda i: (i, 0)),
        out_shape=jax.ShapeDtypeStruct((n, ATTN_W), F32),
        compiler_params=_cparams(("arbitrary",)),
        name="ctx_attn",
    )(qkv)


def _key_row_start(i, n_rows):
    return jnp.minimum(jnp.maximum(Q_ROWS * i - WIN_ROWS // 2, 0), n_rows - KEY_ROWS)


def _natten_bias_blocks(rpb, n_rows):
    col = np.arange(GRID_W)
    dc = np.clip(col[None, :] - col[:, None], -(WIN_COLS - 1), WIN_COLS - 1) + (WIN_COLS - 1)
    col_start = np.clip(col - WIN_COLS // 2, 0, GRID_W - WIN_COLS)
    col_in = (col[None, :] >= col_start[:, None]) & (col[None, :] < col_start[:, None] + WIN_COLS)
    t = jnp.where(jnp.asarray(col_in), rpb[:, :, dc], NEG)
    masked = 2 * WIN_ROWS - 1
    t = jnp.concatenate([t, jnp.full((N_HEADS, 1, GRID_W, GRID_W), NEG, F32)], axis=1)
    n_blk = n_rows // Q_ROWS
    idx = np.full((n_blk, Q_ROWS, KEY_ROWS), masked, np.int32)
    for i in range(n_blk):
        ks = min(max(Q_ROWS * i - WIN_ROWS // 2, 0), n_rows - KEY_ROWS)
        for a in range(Q_ROWS):
            r = Q_ROWS * i + a
            rs = min(max(r - WIN_ROWS // 2, 0), n_rows - WIN_ROWS)
            for j in range(KEY_ROWS):
                if rs <= ks + j < rs + WIN_ROWS:
                    idx[i, a, j] = ks + j - r + (WIN_ROWS - 1)
    b = t[:, idx]
    return b.transpose(0, 1, 2, 4, 3, 5).reshape(N_HEADS, n_blk, Q_ROWS * GRID_W, KEY_ROWS * GRID_W)


def _lat_attn_kernel(q_ref, k_ref, v_ref, kc_ref, vc_ref, bias_ref, o_ref, km_ref, *, n_blk):
    low = lax.broadcasted_iota(jnp.int32, (1, LANES), 1) < HEAD_DIM
    masks = (low, jnp.logical_not(low))
    kb = k_ref[...]
    kc = kc_ref[0]
    vc = vc_ref[0]
    for e in range(2):
        km_ref[e] = jnp.where(masks[e], kb, jnp.zeros_like(kb))
    kce = [jnp.where(masks[e], kc, jnp.zeros_like(kc)) for e in range(2)]
    nq = Q_ROWS * GRID_W
    nk = KEY_ROWS * GRID_W

    def blk_body(i, carry):
        qrows = pl.ds(pl.multiple_of(i * nq, nq), nq)
        krows = pl.ds(pl.multiple_of(_key_row_start(i, Q_ROWS * n_blk) * GRID_W, GRID_W), nk)
        qb = q_ref[qrows, :]
        vw = v_ref[krows, :]
        outs = []
        for e in range(2):
            s_loc = _dot_t(qb, km_ref[e, krows, :]) + bias_ref[e, i]
            s_ctx = _dot_t(qb, kce[e])
            outs.append(_softmax_pv([s_loc, s_ctx], [vw, vc]))
        o_ref[qrows, :] = jnp.where(low, outs[0], outs[1])
        return carry

    lax.fori_loop(0, n_blk, blk_body, 0)


def _lat_attn(qkv, kc, vc, bias, batch, L):
    n_blk = L // (Q_ROWS * GRID_W)
    n_pairs = ATTN_W // LANES
    col = lambda part: pl.BlockSpec((L, LANES), lambda p, b: (b, part * n_pairs + p))
    cache = pl.BlockSpec((1, kc.shape[1], LANES), lambda p, b: (b, 0, p))
    return pl.pallas_call(
        functools.partial(_lat_attn_kernel, n_blk=n_blk),
        grid=(n_pairs, batch),
        in_specs=[col(0), col(1), col(2), cache, cache,
                  pl.BlockSpec((2, n_blk, Q_ROWS * GRID_W, KEY_ROWS * GRID_W), lambda p, b: (p, 0, 0, 0))],
        out_specs=pl.BlockSpec((L, LANES), lambda p, b: (b, p)),
        out_shape=jax.ShapeDtypeStruct((batch * L, ATTN_W), F32),
        scratch_shapes=[pltpu.VMEM((2, L, LANES), BF16)],
        compiler_params=_cparams(("arbitrary", "arbitrary")),
        name="lat_attn",
    )(qkv, qkv, qkv, kc, vc, bias)


def _dwconv3(x, w_ref, b_ref, first, last):
    n = x.shape[0]
    prev = jnp.where(first, 0.0, pltpu.roll(x, 1, 0))
    nxt = jnp.where(last, 0.0, pltpu.roll(x, n - 1, 0))
    return prev * w_ref[0:1, :] + x * w_ref[1:2, :] + nxt * w_ref[2:3, :] + b_ref[...]


def _seq_edges(n, L):
    pos = lax.broadcasted_iota(jnp.int32, (n, 1), 0) & (L - 1)
    return pos == 0, pos == L - 1


def _hyena_kernel(h1_ref, h2_ref, hv_ref, w1_ref, w2_ref, wv_ref, b1_ref, b2_ref, bv_ref,
                  kf_ref, fb_ref, a_ref, ai_ref, o_ref, x1_s, x2_s, v_s, *, L):
    n = h1_ref.shape[0]
    first, last = _seq_edges(n, L)
    x1_s[...] = _dwconv3(h1_ref[...], w1_ref, b1_ref, first, last)
    x2_s[...] = _dwconv3(h2_ref[...], w2_ref, b2_ref, first, last)
    v_s[...] = _dwconv3(hv_ref[...], wv_ref, bv_ref, first, last)
    row0 = lax.broadcasted_iota(jnp.int32, (L, 1), 0) == 0

    def long_conv(u, order):
        ub = u.astype(BF16)
        ur = _dot(a_ref[0:L, :], ub)
        ui = _dot(a_ref[L:2 * L, :], ub)
        kr = kf_ref[order, 0:L, :]
        ki = kf_ref[order, L:2 * L, :]
        yr = jnp.where(row0, ur * kr, ur * kr - ui * ki)
        yi = jnp.where(row0, ui * ki, ur * ki + ui * kr)
        y = _dot(ai_ref[:, 0:L], yr.astype(BF16)) + _dot(ai_ref[:, L:2 * L], yi.astype(BF16))
        return y + u * fb_ref[order:order + 1, :]

    for s in range(n // L):
        rows = slice(s * L, (s + 1) * L)
        z = x1_s[rows, :] * long_conv(v_s[rows, :], 0)
        o_ref[rows, :] = x2_s[rows, :] * long_conv(z, 1)


def _hyena(hy, conv_w, conv_b, kf, filt_bias, a_fwd, a_inv, L):
    n = hy.shape[0]
    ct = 256
    n_ct = HY_CH // ct
    part = lambda k, rows: pl.BlockSpec((rows, ct), lambda c, i: (i if rows == ROW_BLOCK else 0, k * n_ct + c))
    return pl.pallas_call(
        functools.partial(_hyena_kernel, L=L),
        grid=(n_ct, n // ROW_BLOCK),
        in_specs=[part(0, ROW_BLOCK), part(1, ROW_BLOCK), part(2, ROW_BLOCK),
                  part(0, 3), part(1, 3), part(2, 3),
                  part(0, 1), part(1, 1), part(2, 1),
                  pl.BlockSpec((2, 2 * L, ct), lambda c, i: (0, 0, c)),
                  pl.BlockSpec((2, ct), lambda c, i: (0, c)),
                  pl.BlockSpec((2 * L, L), lambda c, i: (0, 0)),
                  pl.BlockSpec((L, 2 * L), lambda c, i: (0, 0))],
        out_specs=pl.BlockSpec((ROW_BLOCK, ct), lambda c, i: (i, c)),
        out_shape=jax.ShapeDtypeStruct((n, HY_CH), F32),
        scratch_shapes=[pltpu.VMEM((ROW_BLOCK, ct), F32)] * 3,
        compiler_params=_cparams(("arbitrary", "arbitrary")),
        name=f"hyena_{L}",
    )(hy, hy, hy, conv_w, conv_w, conv_w, conv_b, conv_b, conv_b, kf, filt_bias, a_fwd, a_inv)


def _tail_kernel(x_ref, a_ref, hy_ref, mods_ref, gg_ref, wo_ref, n2_ref, fg_ref,
                 wg_ref, wu_ref, cwg_ref, cwu_ref, cbg_ref, cbu_ref, wd_ref, o_ref,
                 x1_s, h2_s, acc_s, *, L, mod_base, blocks_per_mod):
    j = pl.program_id(1)
    r = mod_base + pl.program_id(0) // blocks_per_mod
    mod = lambda k: mods_ref[k, pl.ds(r, 1), :]

    @pl.when(j == 0)
    def _():
        gg = gg_ref[...]
        merged = jnp.concatenate([_rms(a_ref[...]) * gg[:, :ATTN_W], _rms(hy_ref[...]) * gg[:, ATTN_W:]],
                                 axis=-1).astype(BF16)
        x1 = x_ref[...] + mod(2) * _dot(merged, wo_ref[...])
        x1_s[...] = x1
        h2_s[...] = (_rms(x1) * n2_ref[...] * (1.0 + mod(4)) + mod(3)).astype(BF16)
        acc_s[...] = jnp.zeros_like(acc_s)

    first, last = _seq_edges(x_ref.shape[0], L)
    h2 = h2_s[...]
    gate = _dwconv3(_dot(h2, wg_ref[...]), cwg_ref, cbg_ref, first, last)
    up = _dwconv3(_dot(h2, wu_ref[...]), cwu_ref, cbu_ref, first, last)
    act = (gate / (1.0 + jnp.exp(-gate)) * up).astype(BF16)
    acc_s[...] += _dot(act, wd_ref[...])

    @pl.when(j == N_FF_TILES - 1)
    def _():
        x2 = x1_s[...] + mod(5) * acc_s[...]
        o_ref[...] = _rms(x2) * fg_ref[...]


def _tail(x, a, hyo, mods, grp_g, w_out, n2_g, final_g, w_up, conv_w, conv_b, w_down, *, L, mod_base,
          blocks_per_mod):
    n = x.shape[0]
    row = lambda w: pl.BlockSpec((ROW_BLOCK, w), lambda i, j: (i, 0))
    vec = pl.BlockSpec((1, D_MODEL), lambda i, j: (0, 0))
    ff = lambda rows, half: pl.BlockSpec((rows, FF_TILE), lambda i, j: (0, half * N_FF_TILES + j))
    return pl.pallas_call(
        functools.partial(_tail_kernel, L=L, mod_base=mod_base, blocks_per_mod=blocks_per_mod),
        grid=(n // ROW_BLOCK, N_FF_TILES),
        in_specs=[row(D_MODEL), row(ATTN_W), row(HY_CH),
                  pl.BlockSpec((6, 8, D_MODEL), lambda i, j: (0, 0, 0)),
                  vec,
                  pl.BlockSpec((D_MODEL, D_MODEL), lambda i, j: (0, 0)),
                  vec, vec,
                  ff(D_MODEL, 0), ff(D_MODEL, 1), ff(3, 0), ff(3, 1), ff(1, 0), ff(1, 1),
                  pl.BlockSpec((FF_TILE, D_MODEL), lambda i, j: (j, 0))],
        out_specs=row(D_MODEL),
        out_shape=jax.ShapeDtypeStruct((n, D_MODEL), F32),
        scratch_shapes=[pltpu.VMEM((ROW_BLOCK, D_MODEL), F32), pltpu.VMEM((ROW_BLOCK, D_MODEL), BF16),
                        pltpu.VMEM((ROW_BLOCK, D_MODEL), F32)],
        compiler_params=_cparams(("arbitrary", "arbitrary")),
        name=f"tail_{L}",
    )(x, a, hyo, mods, grp_g, w_out, n2_g, final_g, w_up, w_up, conv_w, conv_w, conv_b, conv_b, w_down)


def _heads_major(t, batch, L):
    return t.reshape(batch, L, N_HEADS, HEAD_DIM).transpose(0, 2, 1, 3)[:, None]


def kernel(x_prompt, x_sample, cache_ctx_k, cache_ctx_v, c, c_ctx, w_ada, b_ada, norm1_g, w_in, rpb,
           hy_conv_w, hy_conv_b, filt_w1, filt_b1, filt_w2, filt_b2, filt_w3, filt_b3, filt_freq,
           filt_bias, grp_norm_g, w_out, norm2_g, w_up, ffn_conv_w, ffn_conv_b, w_down, final_g):
    assert w_ada.shape[0] == 1, "single-layer trunk"
    bc, lc, _ = x_prompt.shape
    bl, ll, _ = x_sample.shape
    past = cache_ctx_k.shape[3]

    cvec = jnp.concatenate([c_ctx[None], c, jnp.zeros((8 - 1 - bl, D_MODEL), F32)], axis=0)
    mods = _mods(cvec, w_ada[0], b_ada[0])

    w_in_b = w_in[0].astype(BF16)
    w_out_b = w_out[0].astype(BF16)
    w_up_b = w_up[0].astype(BF16)
    w_down_b = w_down[0].astype(BF16)
    g1 = norm1_g[0][None]
    shared_tail = (grp_norm_g[0][None], w_out_b, norm2_g[0][None], final_g[None], w_up_b,
                   ffn_conv_w[0], ffn_conv_b[0][None], w_down_b)
    filt = (filt_w1[0], filt_b1[0], filt_w2[0], filt_b2[0], filt_w3[0], filt_b3[0], filt_freq[0])

    def tables(L):
        fwd, inv = _dft_tables(L)
        return jnp.asarray(fwd).astype(BF16), jnp.asarray(inv).astype(BF16)

    xc = x_prompt.reshape(bc * lc, D_MODEL)
    a_fwd, a_inv = tables(lc)
    kf = _filter_spectrum(lc, a_fwd, *filt)
    qkv, hy, kv = _proj(xc, mods, g1, w_in_b, mod_base=0, blocks_per_mod=bc * lc // PROJ_ROWS, emit_kv=True)
    att = _ctx_attn(qkv, lc)
    hyo = _hyena(hy, hy_conv_w[0], hy_conv_b[0][None], kf, filt_bias[0], a_fwd, a_inv, lc)
    y_prompt = _tail(xc, att, hyo, mods, *shared_tail, L=lc, mod_base=0,
                     blocks_per_mod=bc * lc // ROW_BLOCK).reshape(bc, lc, D_MODEL)
    state_k = _heads_major(kv[:, :ATTN_W], bc, lc)
    state_v = _heads_major(kv[:, ATTN_W:], bc, lc)

    xs = x_sample.reshape(bl * ll, D_MODEL)
    a_fwd, a_inv = tables(ll)
    kf = _filter_spectrum(ll, a_fwd, *filt)
    qkv, hy = _proj(xs, mods, g1, w_in_b, mod_base=1, blocks_per_mod=ll // PROJ_ROWS, emit_kv=False)
    lanes_major = lambda t: t[:, 0].transpose(0, 2, 1, 3).reshape(bl, past, ATTN_W).astype(BF16)
    bias = _natten_bias_blocks(rpb[0], ll // GRID_W)
    att = _lat_attn(qkv, lanes_major(cache_ctx_k), lanes_major(cache_ctx_v), bias, bl, ll)
    hyo = _hyena(hy, hy_conv_w[0], hy_conv_b[0][None], kf, filt_bias[0], a_fwd, a_inv, ll)
    y_sample = _tail(xs, att, hyo, mods, *shared_tail, L=ll, mod_base=1,
                     blocks_per_mod=ll // ROW_BLOCK).reshape(bl, ll, D_MODEL)

    return (y_prompt, y_sample, state_k, state_v)
```

```python
import functools
import math

import numpy as np
import jax
import jax.numpy as jnp
from jax import lax
from jax.experimental import pallas as pl
from jax.experimental.pallas import tpu as pltpu

F32 = jnp.float32
BF16 = jnp.bfloat16
HIGHEST = lax.Precision.HIGHEST

D_MODEL = 1024
N_HEADS = 8
HEAD_DIM = 64
ATTN_W = N_HEADS * HEAD_DIM
HY_CH = D_MODEL - ATTN_W
IN_COLS = 3 * ATTN_W + 3 * HY_CH
D_FF = 2816
GRID_W = 64
WIN_ROWS = 8
WIN_COLS = 16
FILT_FREQS = 8
FILT_HID = 64
FILT_EMB_PAD = 32
DECAY_TARGET = 1e-2
MAX_DECAY = math.log(DECAY_TARGET) / 0.3
MIN_DECAY = math.log(DECAY_TARGET) / 1.5
EPS = 1e-6
NEG = -1e30

LANES = 128
ROW_BLOCK = 1024
PROJ_ROWS = 512
FF_TILE = 256
N_FF_TILES = D_FF // FF_TILE
Q_ROWS = 2
KEY_ROWS = 10
VMEM_LIMIT = 56 * 1024 * 1024


def _cparams(sem):
    return pltpu.CompilerParams(dimension_semantics=sem, vmem_limit_bytes=VMEM_LIMIT)


def _rms(x):
    return x * lax.rsqrt(jnp.mean(x * x, axis=-1, keepdims=True) + EPS)


def _dot(a, b):
    return jnp.dot(a, b, preferred_element_type=F32)


def _dot_t(a, b):
    return lax.dot_general(a, b, (((1,), (1,)), ((), ())), preferred_element_type=F32)


def _mods_kernel(c_ref, w_ref, b_ref, o_ref):
    cv = c_ref[...]
    s = cv / (1.0 + jnp.exp(-cv))
    o_ref[0] = jnp.dot(s, w_ref[...], precision=HIGHEST, preferred_element_type=F32) + b_ref[0]


def _mods(cvec, w_ada, b_ada):
    return pl.pallas_call(
        _mods_kernel,
        grid=(6,),
        in_specs=[
            pl.BlockSpec((8, D_MODEL), lambda j: (0, 0)),
            pl.BlockSpec((D_MODEL, D_MODEL), lambda j: (0, j)),
            pl.BlockSpec((1, 1, D_MODEL), lambda j: (j, 0, 0)),
        ],
        out_specs=pl.BlockSpec((1, 8, D_MODEL), lambda j: (j, 0, 0)),
        out_shape=jax.ShapeDtypeStruct((6, 8, D_MODEL), F32),
        compiler_params=_cparams(("arbitrary",)),
        name="mods",
    )(cvec, w_ada, b_ada.reshape(6, 1, D_MODEL))


def _dft_tables(L):
    f = np.arange(L, dtype=np.int64)[:, None]
    t = np.arange(L, dtype=np.int64)[None, :]
    ang = np.pi * ((f * t) % (2 * L)).astype(np.float64) / L
    top = np.cos(ang)
    bot = -np.sin(ang)
    bot[0, :] = np.where(np.arange(L) % 2 == 0, 1.0, -1.0)
    fwd = np.concatenate([top, bot], axis=0)
    w = np.full((2 * L,), 1.0 / L)
    w[0] = w[L] = 0.5 / L
    inv = (fwd * w[:, None]).T
    return fwd.astype(np.float32), inv.astype(np.float32)


def _filter_consts(L):
    t = np.arange(L, dtype=np.float64) / L
    fr = np.arange(1, FILT_FREQS + 1, dtype=np.float64)
    ang = 2.0 * math.pi * t[:, None] * fr[None, :]
    z = np.zeros((L, FILT_EMB_PAD), np.float64)
    z[:, 0] = t
    z[:, 1:1 + FILT_FREQS] = np.cos(ang)
    z[:, 1 + FILT_FREQS:1 + 2 * FILT_FREQS] = np.sin(ang)
    deltas = np.abs(np.linspace(MIN_DECAY, MAX_DECAY, HY_CH))
    decay = np.exp(-t[:, None] * deltas[None, :])
    return z.astype(np.float32), decay.astype(np.float32)


def _filter_kernel(z_ref, w1_ref, b1_ref, w2_ref, b2_ref, w3_ref, b3_ref, fr_ref, dec_ref,
                   a_ref, o_ref, *, L):
    fr = fr_ref[...]
    h = jnp.sin(fr * (jnp.dot(z_ref[...], w1_ref[...], precision=HIGHEST,
                              preferred_element_type=F32) + b1_ref[...]))
    h = jnp.sin(fr * (jnp.dot(h, w2_ref[...], precision=HIGHEST,
                              preferred_element_type=F32) + b2_ref[...]))
    h = jnp.dot(h, w3_ref[...], precision=HIGHEST, preferred_element_type=F32) + b3_ref[...]
    dec = dec_ref[...]
    row0 = lax.broadcasted_iota(jnp.int32, (L, 1), 0) == 0
    fwd = h[:, :HY_CH] * dec
    bwd = jnp.where(row0, 0.0, h[:, HY_CH:] * dec)
    gp = (fwd + bwd).astype(BF16)
    gm = (fwd - bwd).astype(BF16)
    top = _dot(a_ref[0:L, :], gp)
    bot = _dot(a_ref[L:2 * L, :], gm)
    nyq = _dot(a_ref[L:L + 16, :], gp)[0:1]
    o_ref[0, 0:L, :] = top
    o_ref[0, L:2 * L, :] = jnp.where(row0, nyq, bot)


def _filter_spectrum(L, a_fwd, w1, b1, w2, b2, w3, b3, freq):
    z, decay = _filter_consts(L)
    full = lambda shape: pl.BlockSpec(shape, lambda o: (0,) * len(shape))
    return pl.pallas_call(
        functools.partial(_filter_kernel, L=L),
        grid=(2,),
        in_specs=[
            full((L, FILT_EMB_PAD)),
            full((FILT_EMB_PAD, FILT_HID)),
            full((1, FILT_HID)),
            full((FILT_HID, FILT_HID)),
            full((1, FILT_HID)),
            pl.BlockSpec((FILT_HID, 2 * HY_CH), lambda o: (0, o)),
            pl.BlockSpec((1, 2 * HY_CH), lambda o: (0, o)),
            full((1, FILT_HID)),
            full((L, HY_CH)),
            full((2 * L, L)),
        ],
        out_specs=pl.BlockSpec((1, 2 * L, HY_CH), lambda o: (o, 0, 0)),
        out_shape=jax.ShapeDtypeStruct((2, 2 * L, HY_CH), F32),
        compiler_params=_cparams(("arbitrary",)),
        name=f"filter_{L}",
    )(jnp.asarray(z), jnp.pad(w1, ((0, FILT_EMB_PAD - w1.shape[0]), (0, 0))), b1[None], w2, b2[None],
      w3, b3[None], freq[None], jnp.asarray(decay), a_fwd)


def _store_heads_major(x, tmp_ref, dst_ref, L):
    low = lax.broadcasted_iota(jnp.int32, (1, LANES), 1) < HEAD_DIM
    for p in range(ATTN_W // LANES):
        tmp_ref[p] = x[:, p * LANES:(p + 1) * LANES]
    for s in range(x.shape[0] // L):
        for p in range(ATTN_W // LANES):
            even = tmp_ref[p, pl.ds(s * L, L // 2, stride=2), :]
            odd = tmp_ref[p, pl.ds(s * L + 1, L // 2, stride=2), :]
            dst_ref[s, 2 * p] = jnp.where(low, even, pltpu.roll(odd, HEAD_DIM, 1))
            dst_ref[s, 2 * p + 1] = jnp.where(low, pltpu.roll(even, HEAD_DIM, 1), odd)


def _proj_kernel(x_ref, mods_ref, g_ref, w_ref, qkv_ref, hy_ref, *state, mod_base, blocks_per_mod, L):
    r = mod_base + pl.program_id(0) // blocks_per_mod
    sh1 = mods_ref[0, pl.ds(r, 1), :]
    sc1 = mods_ref[1, pl.ds(r, 1), :]
    h = (_rms(x_ref[...]) * g_ref[...] * (1.0 + sc1) + sh1).astype(BF16)
    q = _dot(h, w_ref[:, 0:ATTN_W])
    qkv_ref[:, 0:ATTN_W] = (q * HEAD_DIM ** -0.5).astype(BF16)
    k = _dot(h, w_ref[:, ATTN_W:2 * ATTN_W])
    qkv_ref[:, ATTN_W:2 * ATTN_W] = k.astype(BF16)
    v = _dot(h, w_ref[:, 2 * ATTN_W:3 * ATTN_W])
    qkv_ref[:, 2 * ATTN_W:3 * ATTN_W] = v.astype(BF16)
    if state:
        ks_ref, vs_ref, tmp_ref = state
        _store_heads_major(k, tmp_ref, ks_ref, L)
        _store_heads_major(v, tmp_ref, vs_ref, L)
    hy_ref[...] = _dot(h, w_ref[:, 3 * ATTN_W:])


def _proj(x, mods, g, w_in, *, mod_base, blocks_per_mod, L, emit_state):
    n = x.shape[0]
    row = lambda w: pl.BlockSpec((PROJ_ROWS, w), lambda i: (i, 0))
    out_specs = [row(3 * ATTN_W), row(3 * HY_CH)]
    out_shape = [jax.ShapeDtypeStruct((n, 3 * ATTN_W), BF16), jax.ShapeDtypeStruct((n, 3 * HY_CH), F32)]
    scratch = []
    if emit_state:
        seqs = PROJ_ROWS // L
        state = pl.BlockSpec((seqs, N_HEADS, L // 2, 2 * HEAD_DIM), lambda i: (i, 0, 0, 0))
        out_specs += [state, state]
        out_shape += [jax.ShapeDtypeStruct((n // L, N_HEADS, L // 2, 2 * HEAD_DIM), F32)] * 2
        scratch = [pltpu.VMEM((ATTN_W // LANES, PROJ_ROWS, LANES), F32)]
    return pl.pallas_call(
        functools.partial(_proj_kernel, mod_base=mod_base, blocks_per_mod=blocks_per_mod, L=L),
        grid=(n // PROJ_ROWS,),
        in_specs=[
            row(D_MODEL),
            pl.BlockSpec((6, 8, D_MODEL), lambda i: (0, 0, 0)),
            pl.BlockSpec((1, D_MODEL), lambda i: (0, 0)),
            pl.BlockSpec((D_MODEL, IN_COLS), lambda i: (0, 0)),
        ],
        out_specs=out_specs,
        out_shape=out_shape,
        scratch_shapes=scratch,
        compiler_params=_cparams(("arbitrary",)),
        name="proj_ctx" if emit_state else "proj_lat",
    )(x, mods, g, w_in)


def _softmax_pv(scores, values):
    mx = functools.reduce(jnp.maximum, [jnp.max(s, axis=-1, keepdims=True) for s in scores])
    ps = [jnp.exp(s - mx) for s in scores]
    den = functools.reduce(jnp.add, [jnp.sum(p, axis=-1, keepdims=True) for p in ps])
    num = functools.reduce(jnp.add, [_dot(p.astype(BF16), v) for p, v in zip(ps, values)])
    return num * (1.0 / den)


def _ctx_attn_kernel(qkv_ref, o_ref, *, n_seq, L):
    low = lax.broadcasted_iota(jnp.int32, (1, LANES), 1) < HEAD_DIM

    def seq_body(s, carry):
        rows = pl.ds(pl.multiple_of(s * L, L), L)
        for p in range(ATTN_W // LANES):
            cols = lambda part: slice(part * ATTN_W + p * LANES, part * ATTN_W + (p + 1) * LANES)
            qb = qkv_ref[rows, cols(0)]
            kb = qkv_ref[rows, cols(1)]
            vb = qkv_ref[rows, cols(2)]
            outs = []
            for head_mask in (low, jnp.logical_not(low)):
                ke = jnp.where(head_mask, kb, jnp.zeros_like(kb))
                outs.append(_softmax_pv([_dot_t(qb, ke)], [vb]))
            o_ref[rows, p * LANES:(p + 1) * LANES] = jnp.where(low, outs[0], outs[1])
        return carry

    lax.fori_loop(0, n_seq, seq_body, 0)


def _ctx_attn(qkv, L):
    n = qkv.shape[0]
    return pl.pallas_call(
        functools.partial(_ctx_attn_kernel, n_seq=ROW_BLOCK // L, L=L),
        grid=(n // ROW_BLOCK,),
        in_specs=[pl.BlockSpec((ROW_BLOCK, 3 * ATTN_W), lambda i: (i, 0))],
        out_specs=pl.BlockSpec((ROW_BLOCK, ATTN_W), lambda i: (i, 0)),
        out_shape=jax.ShapeDtypeStruct((n, ATTN_W), F32),
        compiler_params=_cparams(("arbitrary",)),
        name="ctx_attn",
    )(qkv)


def _key_row_start(i, n_rows):
    return jnp.minimum(jnp.maximum(Q_ROWS * i - WIN_ROWS // 2, 0), n_rows - KEY_ROWS)


N_DR = 2 * WIN_ROWS - 1
N_DC = 2 * WIN_COLS - 1


def _rpb_expand_table():
    d = np.clip(np.arange(LANES) - GRID_W, -(WIN_COLS - 1), WIN_COLS - 1) + (WIN_COLS - 1)
    t = np.zeros((32, LANES), np.float32)
    t[d, np.arange(LANES)] = 1.0
    return t


def _bias_tile_index(n_rows):
    n_blk = n_rows // Q_ROWS
    idx = np.full((n_blk, Q_ROWS, KEY_ROWS), N_DR, np.int32)
    for i in range(n_blk):
        ks = min(max(Q_ROWS * i - WIN_ROWS // 2, 0), n_rows - KEY_ROWS)
        for a in range(Q_ROWS):
            r = Q_ROWS * i + a
            rs = min(max(r - WIN_ROWS // 2, 0), n_rows - WIN_ROWS)
            for j in range(KEY_ROWS):
                if rs <= ks + j < rs + WIN_ROWS:
                    idx[i, a, j] = ks + j - r + (WIN_ROWS - 1)
    return idx


def _build_bias(rpb_ref, ext_ref, tw_ref, bias_ref, n_blk):
    shape = (GRID_W, LANES)
    lane = lax.broadcasted_iota(jnp.int32, shape, 1)
    qc = lax.broadcasted_iota(jnp.int32, shape, 0)
    kc = lane & (GRID_W - 1)
    low = lane < GRID_W
    col_start = jnp.clip(qc - WIN_COLS // 2, 0, GRID_W - WIN_COLS)
    col_in = (kc >= col_start) & (kc < col_start + WIN_COLS)
    idx = _bias_tile_index(Q_ROWS * n_blk)
    for e in range(2):
        g = jnp.dot(rpb_ref[e], ext_ref[...], precision=HIGHEST, preferred_element_type=F32)
        for dr in range(N_DR):
            t = pltpu.roll(jnp.broadcast_to(g[dr:dr + 1, :], shape), GRID_W, 1, stride=1, stride_axis=0)
            t = jnp.where(low, t, pltpu.roll(t, GRID_W, 1))
            tw_ref[e, dr] = jnp.where(col_in, t, NEG)
        tw_ref[e, N_DR] = jnp.full(shape, NEG, F32)
        for i in range(n_blk):
            for a in range(Q_ROWS):
                for jt in range(KEY_ROWS // 2):
                    tile = jnp.where(low, tw_ref[e, int(idx[i, a, 2 * jt])], tw_ref[e, int(idx[i, a, 2 * jt + 1])])
                    bias_ref[e, i, a * GRID_W:(a + 1) * GRID_W, jt * LANES:(jt + 1) * LANES] = tile


def _lat_attn_kernel(q_ref, k_ref, v_ref, kc_ref, vc_ref, rpb_ref, ext_ref, o_ref, km_ref, tw_ref, bias_ref,
                     *, n_blk):
    @pl.when(pl.program_id(1) == 0)
    def _():
        _build_bias(rpb_ref, ext_ref, tw_ref, bias_ref, n_blk)

    low = lax.broadcasted_iota(jnp.int32, (1, LANES), 1) < HEAD_DIM
    masks = (low, jnp.logical_not(low))
    kb = k_ref[...]
    kc = kc_ref[0]
    vc = vc_ref[0]
    for e in range(2):
        km_ref[e] = jnp.where(masks[e], kb, jnp.zeros_like(kb))
    kce = [jnp.where(masks[e], kc, jnp.zeros_like(kc)) for e in range(2)]
    nq = Q_ROWS * GRID_W
    nk = KEY_ROWS * GRID_W

    def blk_body(i, carry):
        qrows = pl.ds(pl.multiple_of(i * nq, nq), nq)
        krows = pl.ds(pl.multiple_of(_key_row_start(i, Q_ROWS * n_blk) * GRID_W, GRID_W), nk)
        qb = q_ref[qrows, :]
        vw = v_ref[krows, :]
        outs = []
        for e in range(2):
            s_loc = _dot_t(qb, km_ref[e, krows, :]) + bias_ref[e, i]
            s_ctx = _dot_t(qb, kce[e])
            outs.append(_softmax_pv([s_loc, s_ctx], [vw, vc]))
        o_ref[qrows, :] = jnp.where(low, outs[0], outs[1])
        return carry

    lax.fori_loop(0, n_blk, blk_body, 0)


def _lat_attn(qkv, kc, vc, rpb, batch, L):
    n_blk = L // (Q_ROWS * GRID_W)
    n_pairs = ATTN_W // LANES
    col = lambda part: pl.BlockSpec((L, LANES), lambda p, b: (b, part * n_pairs + p))
    cache = pl.BlockSpec((1, kc.shape[1], LANES), lambda p, b: (b, 0, p))
    rpb_pad = jnp.pad(rpb, ((0, 0), (0, 16 - N_DR), (0, 32 - N_DC)))
    return pl.pallas_call(
        functools.partial(_lat_attn_kernel, n_blk=n_blk),
        grid=(n_pairs, batch),
        in_specs=[col(0), col(1), col(2), cache, cache,
                  pl.BlockSpec((2, 16, 32), lambda p, b: (p, 0, 0)),
                  pl.BlockSpec((32, LANES), lambda p, b: (0, 0))],
        out_specs=pl.BlockSpec((L, LANES), lambda p, b: (b, p)),
        out_shape=jax.ShapeDtypeStruct((batch * L, ATTN_W), F32),
        scratch_shapes=[pltpu.VMEM((2, L, LANES), BF16),
                        pltpu.VMEM((2, 16, GRID_W, LANES), F32),
                        pltpu.VMEM((2, n_blk, Q_ROWS * GRID_W, KEY_ROWS * GRID_W), F32)],
        compiler_params=_cparams(("arbitrary", "arbitrary")),
        name="lat_attn",
    )(qkv, qkv, qkv, kc, vc, rpb_pad, jnp.asarray(_rpb_expand_table()))


def _dwconv3(x, w_ref, b_ref, first, last):
    n = x.shape[0]
    prev = jnp.where(first, 0.0, pltpu.roll(x, 1, 0))
    nxt = jnp.where(last, 0.0, pltpu.roll(x, n - 1, 0))
    return prev * w_ref[0:1, :] + x * w_ref[1:2, :] + nxt * w_ref[2:3, :] + b_ref[...]


def _seq_edges(n, L):
    pos = lax.broadcasted_iota(jnp.int32, (n, 1), 0) & (L - 1)
    return pos == 0, pos == L - 1


def _hyena_kernel(h1_ref, h2_ref, hv_ref, w1_ref, w2_ref, wv_ref, b1_ref, b2_ref, bv_ref,
                  kf_ref, fb_ref, a_ref, ai_ref, o_ref, x1_s, x2_s, v_s, *, L):
    n = h1_ref.shape[0]
    first, last = _seq_edges(n, L)
    x1_s[...] = _dwconv3(h1_ref[...], w1_ref, b1_ref, first, last)
    x2_s[...] = _dwconv3(h2_ref[...], w2_ref, b2_ref, first, last)
    v_s[...] = _dwconv3(hv_ref[...], wv_ref, bv_ref, first, last)
    row0 = lax.broadcasted_iota(jnp.int32, (L, 1), 0) == 0

    def long_conv(u, order):
        ub = u.astype(BF16)
        ur = _dot(a_ref[0:L, :], ub)
        ui = _dot(a_ref[L:2 * L, :], ub)
        kr = kf_ref[order, 0:L, :]
        ki = kf_ref[order, L:2 * L, :]
        yr = jnp.where(row0, ur * kr, ur * kr - ui * ki)
        yi = jnp.where(row0, ui * ki, ur * ki + ui * kr)
        y = _dot(ai_ref[:, 0:L], yr.astype(BF16)) + _dot(ai_ref[:, L:2 * L], yi.astype(BF16))
        return y + u * fb_ref[order:order + 1, :]

    for s in range(n // L):
        rows = slice(s * L, (s + 1) * L)
        z = x1_s[rows, :] * long_conv(v_s[rows, :], 0)
        o_ref[rows, :] = x2_s[rows, :] * long_conv(z, 1)


def _hyena(hy, conv_w, conv_b, kf, filt_bias, a_fwd, a_inv, L):
    n = hy.shape[0]
    ct = 256
    n_ct = HY_CH // ct
    part = lambda k, rows: pl.BlockSpec((rows, ct), lambda c, i: (i if rows == ROW_BLOCK else 0, k * n_ct + c))
    return pl.pallas_call(
        functools.partial(_hyena_kernel, L=L),
        grid=(n_ct, n // ROW_BLOCK),
        in_specs=[part(0, ROW_BLOCK), part(1, ROW_BLOCK), part(2, ROW_BLOCK),
                  part(0, 3), part(1, 3), part(2, 3),
                  part(0, 1), part(1, 1), part(2, 1),
                  pl.BlockSpec((2, 2 * L, ct), lambda c, i: (0, 0, c)),
                  pl.BlockSpec((2, ct), lambda c, i: (0, c)),
                  pl.BlockSpec((2 * L, L), lambda c, i: (0, 0)),
                  pl.BlockSpec((L, 2 * L), lambda c, i: (0, 0))],
        out_specs=pl.BlockSpec((ROW_BLOCK, ct), lambda c, i: (i, c)),
        out_shape=jax.ShapeDtypeStruct((n, HY_CH), F32),
        scratch_shapes=[pltpu.VMEM((ROW_BLOCK, ct), F32)] * 3,
        compiler_params=_cparams(("arbitrary", "arbitrary")),
        name=f"hyena_{L}",
    )(hy, hy, hy, conv_w, conv_w, conv_w, conv_b, conv_b, conv_b, kf, filt_bias, a_fwd, a_inv)


def _tail_kernel(x_ref, a_ref, hy_ref, mods_ref, gg_ref, wo_ref, n2_ref, fg_ref,
                 wg_ref, wu_ref, cwg_ref, cwu_ref, cbg_ref, cbu_ref, wd_ref, o_ref,
                 x1_s, h2_s, acc_s, *, L, mod_base, blocks_per_mod):
    j = pl.program_id(1)
    r = mod_base + pl.program_id(0) // blocks_per_mod
    mod = lambda k: mods_ref[k, pl.ds(r, 1), :]

    @pl.when(j == 0)
    def _():
        gg = gg_ref[...]
        merged = jnp.concatenate([_rms(a_ref[...]) * gg[:, :ATTN_W], _rms(hy_ref[...]) * gg[:, ATTN_W:]],
                                 axis=-1).astype(BF16)
        x1 = x_ref[...] + mod(2) * _dot(merged, wo_ref[...])
        x1_s[...] = x1
        h2_s[...] = (_rms(x1) * n2_ref[...] * (1.0 + mod(4)) + mod(3)).astype(BF16)
        acc_s[...] = jnp.zeros_like(acc_s)

    first, last = _seq_edges(x_ref.shape[0], L)
    h2 = h2_s[...]
    gate = _dwconv3(_dot(h2, wg_ref[...]), cwg_ref, cbg_ref, first, last)
    up = _dwconv3(_dot(h2, wu_ref[...]), cwu_ref, cbu_ref, first, last)
    act = (gate / (1.0 + jnp.exp(-gate)) * up).astype(BF16)
    acc_s[...] += _dot(act, wd_ref[...])

    @pl.when(j == N_FF_TILES - 1)
    def _():
        x2 = x1_s[...] + mod(5) * acc_s[...]
        o_ref[...] = _rms(x2) * fg_ref[...]


def _tail(x, a, hyo, mods, grp_g, w_out, n2_g, final_g, w_up, conv_w, conv_b, w_down, *, L, mod_base,
          blocks_per_mod):
    n = x.shape[0]
    row = lambda w: pl.BlockSpec((ROW_BLOCK, w), lambda i, j: (i, 0))
    vec = pl.BlockSpec((1, D_MODEL), lambda i, j: (0, 0))
    ff = lambda rows, half: pl.BlockSpec((rows, FF_TILE), lambda i, j: (0, half * N_FF_TILES + j))
    return pl.pallas_call(
        functools.partial(_tail_kernel, L=L, mod_base=mod_base, blocks_per_mod=blocks_per_mod),
        grid=(n // ROW_BLOCK, N_FF_TILES),
        in_specs=[row(D_MODEL), row(ATTN_W), row(HY_CH),
                  pl.BlockSpec((6, 8, D_MODEL), lambda i, j: (0, 0, 0)),
                  vec,
                  pl.BlockSpec((D_MODEL, D_MODEL), lambda i, j: (0, 0)),
                  vec, vec,
                  ff(D_MODEL, 0), ff(D_MODEL, 1), ff(3, 0), ff(3, 1), ff(1, 0), ff(1, 1),
                  pl.BlockSpec((FF_TILE, D_MODEL), lambda i, j: (j, 0))],
        out_specs=row(D_MODEL),
        out_shape=jax.ShapeDtypeStruct((n, D_MODEL), F32),
        scratch_shapes=[pltpu.VMEM((ROW_BLOCK, D_MODEL), F32), pltpu.VMEM((ROW_BLOCK, D_MODEL), BF16),
                        pltpu.VMEM((ROW_BLOCK, D_MODEL), F32)],
        compiler_params=_cparams(("arbitrary", "arbitrary")),
        name=f"tail_{L}",
    )(x, a, hyo, mods, grp_g, w_out, n2_g, final_g, w_up, w_up, conv_w, conv_w, conv_b, conv_b, w_down)


def kernel(x_prompt, x_sample, cache_ctx_k, cache_ctx_v, c, c_ctx, w_ada, b_ada, norm1_g, w_in, rpb,
           hy_conv_w, hy_conv_b, filt_w1, filt_b1, filt_w2, filt_b2, filt_w3, filt_b3, filt_freq,
           filt_bias, grp_norm_g, w_out, norm2_g, w_up, ffn_conv_w, ffn_conv_b, w_down, final_g):
    assert w_ada.shape[0] == 1, "single-layer trunk"
    bc, lc, _ = x_prompt.shape
    bl, ll, _ = x_sample.shape
    past = cache_ctx_k.shape[3]

    cvec = jnp.concatenate([c_ctx[None], c, jnp.zeros((8 - 1 - bl, D_MODEL), F32)], axis=0)
    mods = _mods(cvec, w_ada[0], b_ada[0])

    w_in_b = w_in[0].astype(BF16)
    w_out_b = w_out[0].astype(BF16)
    w_up_b = w_up[0].astype(BF16)
    w_down_b = w_down[0].astype(BF16)
    g1 = norm1_g[0][None]
    shared_tail = (grp_norm_g[0][None], w_out_b, norm2_g[0][None], final_g[None], w_up_b,
                   ffn_conv_w[0], ffn_conv_b[0][None], w_down_b)
    filt = (filt_w1[0], filt_b1[0], filt_w2[0], filt_b2[0], filt_w3[0], filt_b3[0], filt_freq[0])

    def tables(L):
        fwd, inv = _dft_tables(L)
        return jnp.asarray(fwd).astype(BF16), jnp.asarray(inv).astype(BF16)

    xc = x_prompt.reshape(bc * lc, D_MODEL)
    a_fwd, a_inv = tables(lc)
    kf = _filter_spectrum(lc, a_fwd, *filt)
    qkv, hy, state_k, state_v = _proj(xc, mods, g1, w_in_b, mod_base=0, blocks_per_mod=bc * lc // PROJ_ROWS,
                                      L=lc, emit_state=True)
    att = _ctx_attn(qkv, lc)
    hyo = _hyena(hy, hy_conv_w[0], hy_conv_b[0][None], kf, filt_bias[0], a_fwd, a_inv, lc)
    y_prompt = _tail(xc, att, hyo, mods, *shared_tail, L=lc, mod_base=0,
                     blocks_per_mod=bc * lc // ROW_BLOCK).reshape(bc, lc, D_MODEL)
    state_shape = (bc, 1, N_HEADS, lc, HEAD_DIM)
    state_k = state_k.reshape(state_shape)
    state_v = state_v.reshape(state_shape)

    xs = x_sample.reshape(bl * ll, D_MODEL)
    a_fwd, a_inv = tables(ll)
    kf = _filter_spectrum(ll, a_fwd, *filt)
    qkv, hy = _proj(xs, mods, g1, w_in_b, mod_base=1, blocks_per_mod=ll // PROJ_ROWS, L=ll, emit_state=False)
    lanes_major = lambda t: t[:, 0].transpose(0, 2, 1, 3).reshape(bl, past, ATTN_W).astype(BF16)
    att = _lat_attn(qkv, lanes_major(cache_ctx_k), lanes_major(cache_ctx_v), rpb[0], bl, ll)
    hyo = _hyena(hy, hy_conv_w[0], hy_conv_b[0][None], kf, filt_bias[0], a_fwd, a_inv, ll)
    y_sample = _tail(xs, att, hyo, mods, *shared_tail, L=ll, mod_base=1,
                     blocks_per_mod=ll // ROW_BLOCK).reshape(bl, ll, D_MODEL)

    return (y_prompt, y_sample, state_k, state_v)
```

```python
import functools
import math

import numpy as np
import jax
import jax.numpy as jnp
from jax import lax
from jax.experimental import pallas as pl
from jax.experimental.pallas import tpu as pltpu

F32 = jnp.float32
BF16 = jnp.bfloat16
HIGHEST = lax.Precision.HIGHEST

D_MODEL = 1024
N_HEADS = 8
HEAD_DIM = 64
ATTN_W = N_HEADS * HEAD_DIM
HY_CH = D_MODEL - ATTN_W
IN_COLS = 3 * ATTN_W + 3 * HY_CH
D_FF = 2816
GRID_W = 64
WIN_ROWS = 8
WIN_COLS = 16
FILT_FREQS = 8
FILT_HID = 64
FILT_EMB_PAD = 32
DECAY_TARGET = 1e-2
MAX_DECAY = math.log(DECAY_TARGET) / 0.3
MIN_DECAY = math.log(DECAY_TARGET) / 1.5
EPS = 1e-6
NEG = -1e30

LANES = 128
ROW_BLOCK = 1024
PROJ_ROWS = 512
TAIL_ROWS = 512
TAIL_HALO = 16
TAIL_PAD = 8
FF_TILE = 256
N_FF_TILES = D_FF // FF_TILE
Q_ROWS = 2
KEY_ROWS = 10
VMEM_LIMIT = 56 * 1024 * 1024


def _cparams(sem):
    return pltpu.CompilerParams(dimension_semantics=sem, vmem_limit_bytes=VMEM_LIMIT)


def _rms(x):
    return x * lax.rsqrt(jnp.mean(x * x, axis=-1, keepdims=True) + EPS)


def _dot(a, b):
    return jnp.dot(a, b, preferred_element_type=F32)


def _dot_t(a, b):
    return lax.dot_general(a, b, (((1,), (1,)), ((), ())), preferred_element_type=F32)


def _mods_kernel(c_ref, w_ref, b_ref, o_ref):
    cv = c_ref[...]
    s = cv / (1.0 + jnp.exp(-cv))
    o_ref[0] = jnp.dot(s, w_ref[...], precision=HIGHEST, preferred_element_type=F32) + b_ref[0]


def _mods(cvec, w_ada, b_ada):
    return pl.pallas_call(
        _mods_kernel,
        grid=(6,),
        in_specs=[
            pl.BlockSpec((8, D_MODEL), lambda j: (0, 0)),
            pl.BlockSpec((D_MODEL, D_MODEL), lambda j: (0, j)),
            pl.BlockSpec((1, 1, D_MODEL), lambda j: (j, 0, 0)),
        ],
        out_specs=pl.BlockSpec((1, 8, D_MODEL), lambda j: (j, 0, 0)),
        out_shape=jax.ShapeDtypeStruct((6, 8, D_MODEL), F32),
        compiler_params=_cparams(("arbitrary",)),
        name="mods",
    )(cvec, w_ada, b_ada.reshape(6, 1, D_MODEL))


def _dft_tables(L):
    f = np.arange(L, dtype=np.int64)[:, None]
    t = np.arange(L, dtype=np.int64)[None, :]
    ang = np.pi * ((f * t) % (2 * L)).astype(np.float64) / L
    top = np.cos(ang)
    bot = -np.sin(ang)
    bot[0, :] = np.where(np.arange(L) % 2 == 0, 1.0, -1.0)
    fwd = np.concatenate([top, bot], axis=0)
    w = np.full((2 * L,), 1.0 / L)
    w[0] = w[L] = 0.5 / L
    inv = (fwd * w[:, None]).T
    return fwd.astype(np.float32), inv.astype(np.float32)


def _filter_consts(L):
    t = np.arange(L, dtype=np.float64) / L
    fr = np.arange(1, FILT_FREQS + 1, dtype=np.float64)
    ang = 2.0 * math.pi * t[:, None] * fr[None, :]
    z = np.zeros((L, FILT_EMB_PAD), np.float64)
    z[:, 0] = t
    z[:, 1:1 + FILT_FREQS] = np.cos(ang)
    z[:, 1 + FILT_FREQS:1 + 2 * FILT_FREQS] = np.sin(ang)
    deltas = np.abs(np.linspace(MIN_DECAY, MAX_DECAY, HY_CH))
    decay = np.exp(-t[:, None] * deltas[None, :])
    return z.astype(np.float32), decay.astype(np.float32)


def _filter_kernel(z_ref, w1_ref, b1_ref, w2_ref, b2_ref, w3_ref, b3_ref, fr_ref, dec_ref,
                   a_ref, o_ref, *, L):
    fr = fr_ref[...]
    h = jnp.sin(fr * (jnp.dot(z_ref[...], w1_ref[...], precision=HIGHEST,
                              preferred_element_type=F32) + b1_ref[...]))
    h = jnp.sin(fr * (jnp.dot(h, w2_ref[...], precision=HIGHEST,
                              preferred_element_type=F32) + b2_ref[...]))
    h = jnp.dot(h, w3_ref[...], precision=HIGHEST, preferred_element_type=F32) + b3_ref[...]
    dec = dec_ref[...]
    row0 = lax.broadcasted_iota(jnp.int32, (L, 1), 0) == 0
    fwd = h[:, :HY_CH] * dec
    bwd = jnp.where(row0, 0.0, h[:, HY_CH:] * dec)
    gp = (fwd + bwd).astype(BF16)
    gm = (fwd - bwd).astype(BF16)
    top = _dot(a_ref[0:L, :], gp)
    bot = _dot(a_ref[L:2 * L, :], gm)
    nyq = _dot(a_ref[L:L + 16, :], gp)[0:1]
    o_ref[0, 0:L, :] = top
    o_ref[0, L:2 * L, :] = jnp.where(row0, nyq, bot)


def _filter_spectrum(L, a_fwd, w1, b1, w2, b2, w3, b3, freq):
    z, decay = _filter_consts(L)
    full = lambda shape: pl.BlockSpec(shape, lambda o: (0,) * len(shape))
    return pl.pallas_call(
        functools.partial(_filter_kernel, L=L),
        grid=(2,),
        in_specs=[
            full((L, FILT_EMB_PAD)),
            full((FILT_EMB_PAD, FILT_HID)),
            full((1, FILT_HID)),
            full((FILT_HID, FILT_HID)),
            full((1, FILT_HID)),
            pl.BlockSpec((FILT_HID, 2 * HY_CH), lambda o: (0, o)),
            pl.BlockSpec((1, 2 * HY_CH), lambda o: (0, o)),
            full((1, FILT_HID)),
            full((L, HY_CH)),
            full((2 * L, L)),
        ],
        out_specs=pl.BlockSpec((1, 2 * L, HY_CH), lambda o: (o, 0, 0)),
        out_shape=jax.ShapeDtypeStruct((2, 2 * L, HY_CH), F32),
        compiler_params=_cparams(("arbitrary",)),
        name=f"filter_{L}",
    )(jnp.asarray(z), jnp.pad(w1, ((0, FILT_EMB_PAD - w1.shape[0]), (0, 0))), b1[None], w2, b2[None],
      w3, b3[None], freq[None], jnp.asarray(decay), a_fwd)


def _store_heads_major(x, dst_ref, L):
    for s in range(x.shape[0] // L):
        for h in range(N_HEADS):
            dst_ref[s, 0, h] = x[s * L:(s + 1) * L, h * HEAD_DIM:(h + 1) * HEAD_DIM]


def _proj_kernel(x_ref, mods_ref, g_ref, w_ref, qkv_ref, hy_ref, *state, mod_base, blocks_per_mod, L):
    r = mod_base + pl.program_id(0) // blocks_per_mod
    sh1 = mods_ref[0, pl.ds(r, 1), :]
    sc1 = mods_ref[1, pl.ds(r, 1), :]
    h = (_rms(x_ref[...]) * g_ref[...] * (1.0 + sc1) + sh1).astype(BF16)
    q = _dot(h, w_ref[:, 0:ATTN_W])
    qkv_ref[:, 0:ATTN_W] = (q * HEAD_DIM ** -0.5).astype(BF16)
    k = _dot(h, w_ref[:, ATTN_W:2 * ATTN_W])
    qkv_ref[:, ATTN_W:2 * ATTN_W] = k.astype(BF16)
    v = _dot(h, w_ref[:, 2 * ATTN_W:3 * ATTN_W])
    qkv_ref[:, 2 * ATTN_W:3 * ATTN_W] = v.astype(BF16)
    if state:
        ks_ref, vs_ref = state
        _store_heads_major(k, ks_ref, L)
        _store_heads_major(v, vs_ref, L)
    hy_ref[...] = _dot(h, w_ref[:, 3 * ATTN_W:])


def _proj(x, mods, g, w_in, *, mod_base, blocks_per_mod, L, emit_state):
    n = x.shape[0]
    row = lambda w: pl.BlockSpec((PROJ_ROWS, w), lambda i: (i, 0))
    out_specs = [row(3 * ATTN_W), row(3 * HY_CH)]
    out_shape = [jax.ShapeDtypeStruct((n, 3 * ATTN_W), BF16), jax.ShapeDtypeStruct((n, 3 * HY_CH), F32)]
    if emit_state:
        seqs = PROJ_ROWS // L
        state = pl.BlockSpec((seqs, 1, N_HEADS, L, HEAD_DIM), lambda i: (i, 0, 0, 0, 0))
        out_specs += [state, state]
        out_shape += [jax.ShapeDtypeStruct((n // L, 1, N_HEADS, L, HEAD_DIM), F32)] * 2
    return pl.pallas_call(
        functools.partial(_proj_kernel, mod_base=mod_base, blocks_per_mod=blocks_per_mod, L=L),
        grid=(n // PROJ_ROWS,),
        in_specs=[
            row(D_MODEL),
            pl.BlockSpec((6, 8, D_MODEL), lambda i: (0, 0, 0)),
            pl.BlockSpec((1, D_MODEL), lambda i: (0, 0)),
            pl.BlockSpec((D_MODEL, IN_COLS), lambda i: (0, 0)),
        ],
        out_specs=out_specs,
        out_shape=out_shape,
        compiler_params=_cparams(("arbitrary",)),
        name="proj_ctx" if emit_state else "proj_lat",
    )(x, mods, g, w_in)


def _softmax_pv(scores, values):
    mx = functools.reduce(jnp.maximum, [jnp.max(s, axis=-1, keepdims=True) for s in scores])
    ps = [jnp.exp(s - mx) for s in scores]
    den = functools.reduce(jnp.add, [jnp.sum(p, axis=-1, keepdims=True) for p in ps])
    num = functools.reduce(jnp.add, [_dot(p.astype(BF16), v) for p, v in zip(ps, values)])
    return num * (1.0 / den)


def _ctx_attn_kernel(qkv_ref, o_ref, *, n_seq, L):
    low = lax.broadcasted_iota(jnp.int32, (1, LANES), 1) < HEAD_DIM

    def seq_body(s, carry):
        rows = pl.ds(pl.multiple_of(s * L, L), L)
        for p in range(ATTN_W // LANES):
            cols = lambda part: slice(part * ATTN_W + p * LANES, part * ATTN_W + (p + 1) * LANES)
            qb = qkv_ref[rows, cols(0)]
            kb = qkv_ref[rows, cols(1)]
            vb = qkv_ref[rows, cols(2)]
            outs = []
            for head_mask in (low, jnp.logical_not(low)):
                ke = jnp.where(head_mask, kb, jnp.zeros_like(kb))
                outs.append(_softmax_pv([_dot_t(qb, ke)], [vb]))
            o_ref[rows, p * LANES:(p + 1) * LANES] = jnp.where(low, outs[0], outs[1])
        return carry

    lax.fori_loop(0, n_seq, seq_body, 0)


def _ctx_attn(qkv, L):
    n = qkv.shape[0]
    return pl.pallas_call(
        functools.partial(_ctx_attn_kernel, n_seq=ROW_BLOCK // L, L=L),
        grid=(n // ROW_BLOCK,),
        in_specs=[pl.BlockSpec((ROW_BLOCK, 3 * ATTN_W), lambda i: (i, 0))],
        out_specs=pl.BlockSpec((ROW_BLOCK, ATTN_W), lambda i: (i, 0)),
        out_shape=jax.ShapeDtypeStruct((n, ATTN_W), F32),
        compiler_params=_cparams(("arbitrary",)),
        name="ctx_attn",
    )(qkv)


def _key_row_start(i, n_rows):
    return jnp.minimum(jnp.maximum(Q_ROWS * i - WIN_ROWS // 2, 0), n_rows - KEY_ROWS)


N_DR = 2 * WIN_ROWS - 1
N_DC = 2 * WIN_COLS - 1


def _rpb_expand_table():
    d = np.clip(np.arange(LANES) - GRID_W, -(WIN_COLS - 1), WIN_COLS - 1) + (WIN_COLS - 1)
    t = np.zeros((32, LANES), np.float32)
    t[d, np.arange(LANES)] = 1.0
    return t


def _bias_tile_index(n_rows):
    n_blk = n_rows // Q_ROWS
    idx = np.full((n_blk, Q_ROWS, KEY_ROWS), N_DR, np.int32)
    for i in range(n_blk):
        ks = min(max(Q_ROWS * i - WIN_ROWS // 2, 0), n_rows - KEY_ROWS)
        for a in range(Q_ROWS):
            r = Q_ROWS * i + a
            rs = min(max(r - WIN_ROWS // 2, 0), n_rows - WIN_ROWS)
            for j in range(KEY_ROWS):
                if rs <= ks + j < rs + WIN_ROWS:
                    idx[i, a, j] = ks + j - r + (WIN_ROWS - 1)
    return idx


def _build_bias(rpb_ref, ext_ref, tw_ref, bias_ref, n_blk):
    shape = (GRID_W, LANES)
    lane = lax.broadcasted_iota(jnp.int32, shape, 1)
    qc = lax.broadcasted_iota(jnp.int32, shape, 0)
    kc = lane & (GRID_W - 1)
    low = lane < GRID_W
    col_start = jnp.clip(qc - WIN_COLS // 2, 0, GRID_W - WIN_COLS)
    col_in = (kc >= col_start) & (kc < col_start + WIN_COLS)
    idx = _bias_tile_index(Q_ROWS * n_blk)
    for e in range(2):
        g = jnp.dot(rpb_ref[e], ext_ref[...], precision=HIGHEST, preferred_element_type=F32)
        for dr in range(N_DR):
            t = pltpu.roll(jnp.broadcast_to(g[dr:dr + 1, :], shape), GRID_W, 1, stride=1, stride_axis=0)
            t = jnp.where(low, t, pltpu.roll(t, GRID_W, 1))
            tw_ref[e, dr] = jnp.where(col_in, t, NEG)
        tw_ref[e, N_DR] = jnp.full(shape, NEG, F32)
        for i in range(n_blk):
            for a in range(Q_ROWS):
                for jt in range(KEY_ROWS // 2):
                    tile = jnp.where(low, tw_ref[e, int(idx[i, a, 2 * jt])], tw_ref[e, int(idx[i, a, 2 * jt + 1])])
                    bias_ref[e, i, a * GRID_W:(a + 1) * GRID_W, jt * LANES:(jt + 1) * LANES] = tile


def _lat_attn_kernel(q_ref, k_ref, v_ref, kc_ref, vc_ref, rpb_ref, ext_ref, o_ref, km_ref, tw_ref, bias_ref,
                     *, n_blk):
    @pl.when(pl.program_id(1) == 0)
    def _():
        _build_bias(rpb_ref, ext_ref, tw_ref, bias_ref, n_blk)

    low = lax.broadcasted_iota(jnp.int32, (1, LANES), 1) < HEAD_DIM
    masks = (low, jnp.logical_not(low))
    kb = k_ref[...]
    kc = kc_ref[0]
    vc = vc_ref[0]
    for e in range(2):
        km_ref[e] = jnp.where(masks[e], kb, jnp.zeros_like(kb))
    kce = [jnp.where(masks[e], kc, jnp.zeros_like(kc)) for e in range(2)]
    nq = Q_ROWS * GRID_W
    nk = KEY_ROWS * GRID_W

    def blk_body(i, carry):
        qrows = pl.ds(pl.multiple_of(i * nq, nq), nq)
        krows = pl.ds(pl.multiple_of(_key_row_start(i, Q_ROWS * n_blk) * GRID_W, GRID_W), nk)
        qb = q_ref[qrows, :]
        vw = v_ref[krows, :]
        outs = []
        for e in range(2):
            s_loc = _dot_t(qb, km_ref[e, krows, :]) + bias_ref[e, i]
            s_ctx = _dot_t(qb, kce[e])
            outs.append(_softmax_pv([s_loc, s_ctx], [vw, vc]))
        o_ref[qrows, :] = jnp.where(low, outs[0], outs[1])
        return carry

    lax.fori_loop(0, n_blk, blk_body, 0)


def _lat_attn(qkv, kc, vc, rpb, batch, L):
    n_blk = L // (Q_ROWS * GRID_W)
    n_pairs = ATTN_W // LANES
    col = lambda part: pl.BlockSpec((L, LANES), lambda p, b: (b, part * n_pairs + p))
    cache = pl.BlockSpec((1, kc.shape[1], LANES), lambda p, b: (b, 0, p))
    rpb_pad = jnp.pad(rpb, ((0, 0), (0, 16 - N_DR), (0, 32 - N_DC)))
    return pl.pallas_call(
        functools.partial(_lat_attn_kernel, n_blk=n_blk),
        grid=(n_pairs, batch),
        in_specs=[col(0), col(1), col(2), cache, cache,
                  pl.BlockSpec((2, 16, 32), lambda p, b: (p, 0, 0)),
                  pl.BlockSpec((32, LANES), lambda p, b: (0, 0))],
        out_specs=pl.BlockSpec((L, LANES), lambda p, b: (b, p)),
        out_shape=jax.ShapeDtypeStruct((batch * L, ATTN_W), F32),
        scratch_shapes=[pltpu.VMEM((2, L, LANES), BF16),
                        pltpu.VMEM((2, 16, GRID_W, LANES), F32),
                        pltpu.VMEM((2, n_blk, Q_ROWS * GRID_W, KEY_ROWS * GRID_W), F32)],
        compiler_params=_cparams(("arbitrary", "arbitrary")),
        name="lat_attn",
    )(qkv, qkv, qkv, kc, vc, rpb_pad, jnp.asarray(_rpb_expand_table()))


def _dwconv3(x, w_ref, b_ref, first, last):
    n = x.shape[0]
    prev = jnp.where(first, 0.0, pltpu.roll(x, 1, 0))
    nxt = jnp.where(last, 0.0, pltpu.roll(x, n - 1, 0))
    return prev * w_ref[0:1, :] + x * w_ref[1:2, :] + nxt * w_ref[2:3, :] + b_ref[...]


def _seq_edges(n, L):
    pos = lax.broadcasted_iota(jnp.int32, (n, 1), 0) & (L - 1)
    return pos == 0, pos == L - 1


def _hyena_kernel(h1_ref, h2_ref, hv_ref, w1_ref, w2_ref, wv_ref, b1_ref, b2_ref, bv_ref,
                  kf_ref, fb_ref, a_ref, ai_ref, o_ref, x1_s, x2_s, v_s, *, L):
    n = h1_ref.shape[0]
    first, last = _seq_edges(n, L)
    x1_s[...] = _dwconv3(h1_ref[...], w1_ref, b1_ref, first, last)
    x2_s[...] = _dwconv3(h2_ref[...], w2_ref, b2_ref, first, last)
    v_s[...] = _dwconv3(hv_ref[...], wv_ref, bv_ref, first, last)
    row0 = lax.broadcasted_iota(jnp.int32, (L, 1), 0) == 0

    def long_conv(u, order):
        ub = u.astype(BF16)
        ur = _dot(a_ref[0:L, :], ub)
        ui = _dot(a_ref[L:2 * L, :], ub)
        kr = kf_ref[order, 0:L, :]
        ki = kf_ref[order, L:2 * L, :]
        yr = jnp.where(row0, ur * kr, ur * kr - ui * ki)
        yi = jnp.where(row0, ui * ki, ur * ki + ui * kr)
        y = _dot(ai_ref[:, 0:L], yr.astype(BF16)) + _dot(ai_ref[:, L:2 * L], yi.astype(BF16))
        return y + u * fb_ref[order:order + 1, :]

    for s in range(n // L):
        rows = slice(s * L, (s + 1) * L)
        z = x1_s[rows, :] * long_conv(v_s[rows, :], 0)
        o_ref[rows, :] = x2_s[rows, :] * long_conv(z, 1)


def _hyena(hy, conv_w, conv_b, kf, filt_bias, a_fwd, a_inv, L):
    n = hy.shape[0]
    ct = 256
    n_ct = HY_CH // ct
    part = lambda k, rows: pl.BlockSpec((rows, ct), lambda c, i: (i if rows == ROW_BLOCK else 0, k * n_ct + c))
    return pl.pallas_call(
        functools.partial(_hyena_kernel, L=L),
        grid=(n_ct, n // ROW_BLOCK),
        in_specs=[part(0, ROW_BLOCK), part(1, ROW_BLOCK), part(2, ROW_BLOCK),
                  part(0, 3), part(1, 3), part(2, 3),
                  part(0, 1), part(1, 1), part(2, 1),
                  pl.BlockSpec((2, 2 * L, ct), lambda c, i: (0, 0, c)),
                  pl.BlockSpec((2, ct), lambda c, i: (0, c)),
                  pl.BlockSpec((2 * L, L), lambda c, i: (0, 0)),
                  pl.BlockSpec((L, 2 * L), lambda c, i: (0, 0))],
        out_specs=pl.BlockSpec((ROW_BLOCK, ct), lambda c, i: (i, c)),
        out_shape=jax.ShapeDtypeStruct((n, HY_CH), F32),
        scratch_shapes=[pltpu.VMEM((ROW_BLOCK, ct), F32)] * 3,
        compiler_params=_cparams(("arbitrary", "arbitrary")),
        name=f"hyena_{L}",
    )(hy, hy, hy, conv_w, conv_w, conv_w, conv_b, conv_b, conv_b, kf, filt_bias, a_fwd, a_inv)


def _tail_kernel(*refs, L, mod_base, blocks_per_mod, halo):
    rows_refs, rest = refs[:9 if halo else 3], refs[9 if halo else 3:]
    (mods_ref, gg_ref, wo_ref, n2_ref, fg_ref, wg_ref, wu_ref, cw_ref, cb_ref, wd_ref, o_ref,
     h2_s, acc_s, ma_s, mb_s) = rest
    rows = o_ref.shape[0]
    i = pl.program_id(0)
    r = mod_base + i // blocks_per_mod
    mod = lambda k: mods_ref[k, pl.ds(r, 1), :]

    if halo:
        x_ref, a_ref, hy_ref, xp_ref, ap_ref, hp_ref, xn_ref, an_ref, hn_ref = rows_refs
        cat = lambda p, m, n: jnp.concatenate([p[...], m[...], n[...]], axis=0)
        x, a, hy = cat(xp_ref, x_ref, xn_ref), cat(ap_ref, a_ref, an_ref), cat(hp_ref, hy_ref, hn_ref)
    else:
        x, a, hy = (ref[...] for ref in rows_refs)
    gg = gg_ref[...]
    merged = jnp.concatenate([_rms(a) * gg[:, :ATTN_W], _rms(hy) * gg[:, ATTN_W:]], axis=-1).astype(BF16)
    x1 = x + mod(2) * _dot(merged, wo_ref[...])
    h2_s[...] = (_rms(x1) * n2_ref[...] * (1.0 + mod(4)) + mod(3)).astype(BF16)
    o_ref[...] = x1[halo:halo + rows]

    pos = (lax.broadcasted_iota(jnp.int32, (rows, 1), 0) + i * rows) & (L - 1)
    first, last = pos == 0, pos == L - 1
    ext = rows + 2 * halo
    base = TAIL_PAD + halo

    def up_proj(j, dst):
        dst[0, TAIL_PAD:TAIL_PAD + ext, :] = _dot(h2_s[...], wg_ref[j])
        dst[1, TAIL_PAD:TAIL_PAD + ext, :] = _dot(h2_s[...], wu_ref[j])

    def conv(src, e, j):
        w = cw_ref.at[e, j]
        prev = jnp.where(first, 0.0, src[e, base - 1:base - 1 + rows, :])
        nxt = jnp.where(last, 0.0, src[e, base + 1:base + 1 + rows, :])
        return prev * w[0:1, :] + src[e, base:base + rows, :] * w[1:2, :] + nxt * w[2:3, :] + cb_ref[e, j]

    def down_proj(j, src):
        gate = conv(src, 0, j)
        act = gate / (1.0 + jnp.exp(-gate)) * conv(src, 1, j)
        return _dot(act.astype(BF16), wd_ref[j])

    for buf in (ma_s, mb_s):
        for e in range(2):
            buf[e, 0:TAIL_PAD, :] = jnp.zeros((TAIL_PAD, FF_TILE), F32)
            buf[e, TAIL_PAD + ext:, :] = jnp.zeros((TAIL_PAD, FF_TILE), F32)
    up_proj(0, ma_s)
    acc_s[...] = jnp.zeros_like(acc_s)

    def two_tiles(k, carry):
        j = 2 * k
        up_proj(j + 1, mb_s)
        acc_s[...] += down_proj(j, ma_s)
        up_proj(j + 2, ma_s)
        acc_s[...] += down_proj(j + 1, mb_s)
        return carry

    lax.fori_loop(0, (N_FF_TILES - 1) // 2, two_tiles, 0)
    x2 = o_ref[...] + mod(5) * (acc_s[...] + down_proj(N_FF_TILES - 1, ma_s))
    o_ref[...] = _rms(x2) * fg_ref[...]


def _tail(x, a, hyo, mods, grp_g, w_out, n2_g, final_g, w_gate, w_upp, conv_w, conv_b, w_down, *, L, mod_base,
          blocks_per_mod):
    n = x.shape[0]
    halo = TAIL_HALO if L > TAIL_ROWS else 0
    row = lambda w: pl.BlockSpec((TAIL_ROWS, w), lambda i: (i, 0))
    per_halo = TAIL_ROWS // TAIL_HALO
    prev = lambda w: pl.BlockSpec((TAIL_HALO, w), lambda i: (jnp.maximum(i * per_halo - 1, 0), 0))
    nxt = lambda w: pl.BlockSpec((TAIL_HALO, w), lambda i: (jnp.minimum((i + 1) * per_halo, n // TAIL_HALO - 1), 0))
    once = lambda shape: pl.BlockSpec(shape, lambda i: (0,) * len(shape), pipeline_mode=pl.Buffered(1))
    widths = (D_MODEL, ATTN_W, HY_CH)
    row_specs = [row(w) for w in widths]
    row_args = [x, a, hyo]
    if halo:
        row_specs += [prev(w) for w in widths] + [nxt(w) for w in widths]
        row_args += [x, a, hyo, x, a, hyo]
    ext = TAIL_ROWS + 2 * halo
    return pl.pallas_call(
        functools.partial(_tail_kernel, L=L, mod_base=mod_base, blocks_per_mod=blocks_per_mod, halo=halo),
        grid=(n // TAIL_ROWS,),
        in_specs=row_specs + [
            once((6, 8, D_MODEL)), once((1, D_MODEL)), once((D_MODEL, D_MODEL)), once((1, D_MODEL)),
            once((1, D_MODEL)),
            once((N_FF_TILES, D_MODEL, FF_TILE)), once((N_FF_TILES, D_MODEL, FF_TILE)),
            once((2, N_FF_TILES, 3, FF_TILE)), once((2, N_FF_TILES, 1, FF_TILE)),
            once((N_FF_TILES, FF_TILE, D_MODEL))],
        out_specs=row(D_MODEL),
        out_shape=jax.ShapeDtypeStruct((n, D_MODEL), F32),
        scratch_shapes=[pltpu.VMEM((ext, D_MODEL), BF16), pltpu.VMEM((TAIL_ROWS, D_MODEL), F32),
                        pltpu.VMEM((2, ext + 2 * TAIL_PAD, FF_TILE), F32),
                        pltpu.VMEM((2, ext + 2 * TAIL_PAD, FF_TILE), F32)],
        compiler_params=_cparams(("arbitrary",)),
        name=f"tail_{L}",
    )(*row_args, mods, grp_g, w_out, n2_g, final_g, w_gate, w_upp, conv_w, conv_b, w_down)


def kernel(x_prompt, x_sample, cache_ctx_k, cache_ctx_v, c, c_ctx, w_ada, b_ada, norm1_g, w_in, rpb,
           hy_conv_w, hy_conv_b, filt_w1, filt_b1, filt_w2, filt_b2, filt_w3, filt_b3, filt_freq,
           filt_bias, grp_norm_g, w_out, norm2_g, w_up, ffn_conv_w, ffn_conv_b, w_down, final_g):
    assert w_ada.shape[0] == 1, "single-layer trunk"
    bc, lc, _ = x_prompt.shape
    bl, ll, _ = x_sample.shape
    past = cache_ctx_k.shape[3]

    cvec = jnp.concatenate([c_ctx[None], c, jnp.zeros((8 - 1 - bl, D_MODEL), F32)], axis=0)
    mods = _mods(cvec, w_ada[0], b_ada[0])

    w_in_b = w_in[0].astype(BF16)
    w_out_b = w_out[0].astype(BF16)
    w_up_t = w_up[0].astype(BF16).reshape(D_MODEL, 2, N_FF_TILES, FF_TILE).transpose(1, 2, 0, 3)
    w_down_t = w_down[0].astype(BF16).reshape(N_FF_TILES, FF_TILE, D_MODEL)
    conv_w_t = ffn_conv_w[0].reshape(3, 2, N_FF_TILES, FF_TILE).transpose(1, 2, 0, 3)
    conv_b_t = ffn_conv_b[0].reshape(2, N_FF_TILES, 1, FF_TILE)
    g1 = norm1_g[0][None]
    shared_tail = (grp_norm_g[0][None], w_out_b, norm2_g[0][None], final_g[None], w_up_t[0], w_up_t[1],
                   conv_w_t, conv_b_t, w_down_t)
    filt = (filt_w1[0], filt_b1[0], filt_w2[0], filt_b2[0], filt_w3[0], filt_b3[0], filt_freq[0])

    def tables(L):
        fwd, inv = _dft_tables(L)
        return jnp.asarray(fwd).astype(BF16), jnp.asarray(inv).astype(BF16)

    xc = x_prompt.reshape(bc * lc, D_MODEL)
    a_fwd, a_inv = tables(lc)
    kf = _filter_spectrum(lc, a_fwd, *filt)
    qkv, hy, state_k, state_v = _proj(xc, mods, g1, w_in_b, mod_base=0, blocks_per_mod=bc * lc // PROJ_ROWS,
                                      L=lc, emit_state=True)
    att = _ctx_attn(qkv, lc)
    hyo = _hyena(hy, hy_conv_w[0], hy_conv_b[0][None], kf, filt_bias[0], a_fwd, a_inv, lc)
    y_prompt = _tail(xc, att, hyo, mods, *shared_tail, L=lc, mod_base=0,
                     blocks_per_mod=bc * lc // TAIL_ROWS).reshape(bc, lc, D_MODEL)

    xs = x_sample.reshape(bl * ll, D_MODEL)
    a_fwd, a_inv = tables(ll)
    kf = _filter_spectrum(ll, a_fwd, *filt)
    qkv, hy = _proj(xs, mods, g1, w_in_b, mod_base=1, blocks_per_mod=ll // PROJ_ROWS, L=ll, emit_state=False)
    lanes_major = lambda t: t[:, 0].transpose(0, 2, 1, 3).reshape(bl, past, ATTN_W).astype(BF16)
    att = _lat_attn(qkv, lanes_major(cache_ctx_k), lanes_major(cache_ctx_v), rpb[0], bl, ll)
    hyo = _hyena(hy, hy_conv_w[0], hy_conv_b[0][None], kf, filt_bias[0], a_fwd, a_inv, ll)
    y_sample = _tail(xs, att, hyo, mods, *shared_tail, L=ll, mod_base=1,
                     blocks_per_mod=ll // TAIL_ROWS).reshape(bl, ll, D_MODEL)

    return (y_prompt, y_sample, state_k, state_v)
```

```python
import functools
import math

import numpy as np
import jax
import jax.numpy as jnp
from jax import lax
from jax.experimental import pallas as pl
from jax.experimental.pallas import tpu as pltpu

F32 = jnp.float32
BF16 = jnp.bfloat16
HIGHEST = lax.Precision.HIGHEST

D_MODEL = 1024
N_HEADS = 8
HEAD_DIM = 64
ATTN_W = N_HEADS * HEAD_DIM
HY_CH = D_MODEL - ATTN_W
IN_COLS = 3 * ATTN_W + 3 * HY_CH
D_FF = 2816
GRID_W = 64
WIN_ROWS = 8
WIN_COLS = 16
FILT_FREQS = 8
FILT_HID = 64
FILT_EMB_PAD = 32
DECAY_TARGET = 1e-2
MAX_DECAY = math.log(DECAY_TARGET) / 0.3
MIN_DECAY = math.log(DECAY_TARGET) / 1.5
EPS = 1e-6
NEG = -1e30

LANES = 128
ROW_BLOCK = 1024
PROJ_ROWS = 512
TAIL_ROWS = 512
TAIL_HALO = 16
TAIL_PAD = 8
FF_TILE = 256
N_FF_TILES = D_FF // FF_TILE
Q_ROWS = 4
KEY_ROWS = 12
VMEM_LIMIT = 56 * 1024 * 1024


def _cparams(sem):
    return pltpu.CompilerParams(dimension_semantics=sem, vmem_limit_bytes=VMEM_LIMIT)


def _rms(x):
    return x * lax.rsqrt(jnp.mean(x * x, axis=-1, keepdims=True) + EPS)


def _dot(a, b):
    return jnp.dot(a, b, preferred_element_type=F32)


def _dot_t(a, b):
    return lax.dot_general(a, b, (((1,), (1,)), ((), ())), preferred_element_type=F32)


def _mods_kernel(c_ref, w_ref, b_ref, o_ref):
    cv = c_ref[...]
    s = cv / (1.0 + jnp.exp(-cv))
    o_ref[0] = jnp.dot(s, w_ref[...], precision=HIGHEST, preferred_element_type=F32) + b_ref[0]


def _mods(cvec, w_ada, b_ada):
    return pl.pallas_call(
        _mods_kernel,
        grid=(6,),
        in_specs=[
            pl.BlockSpec((8, D_MODEL), lambda j: (0, 0)),
            pl.BlockSpec((D_MODEL, D_MODEL), lambda j: (0, j)),
            pl.BlockSpec((1, 1, D_MODEL), lambda j: (j, 0, 0)),
        ],
        out_specs=pl.BlockSpec((1, 8, D_MODEL), lambda j: (j, 0, 0)),
        out_shape=jax.ShapeDtypeStruct((6, 8, D_MODEL), F32),
        compiler_params=_cparams(("arbitrary",)),
        name="mods",
    )(cvec, w_ada, b_ada.reshape(6, 1, D_MODEL))


def _dft_tables(L):
    f = np.arange(L, dtype=np.int64)[:, None]
    t = np.arange(L, dtype=np.int64)[None, :]
    ang = np.pi * ((f * t) % (2 * L)).astype(np.float64) / L
    top = np.cos(ang)
    bot = -np.sin(ang)
    bot[0, :] = np.where(np.arange(L) % 2 == 0, 1.0, -1.0)
    fwd = np.concatenate([top, bot], axis=0)
    w = np.full((2 * L,), 1.0 / L)
    w[0] = w[L] = 0.5 / L
    inv = (fwd * w[:, None]).T
    return fwd.astype(np.float32), inv.astype(np.float32)


def _filter_consts(L):
    t = np.arange(L, dtype=np.float64) / L
    fr = np.arange(1, FILT_FREQS + 1, dtype=np.float64)
    ang = 2.0 * math.pi * t[:, None] * fr[None, :]
    z = np.zeros((L, FILT_EMB_PAD), np.float64)
    z[:, 0] = t
    z[:, 1:1 + FILT_FREQS] = np.cos(ang)
    z[:, 1 + FILT_FREQS:1 + 2 * FILT_FREQS] = np.sin(ang)
    deltas = np.abs(np.linspace(MIN_DECAY, MAX_DECAY, HY_CH))
    decay = np.exp(-t[:, None] * deltas[None, :])
    return z.astype(np.float32), decay.astype(np.float32)


def _filter_kernel(z_ref, w1_ref, b1_ref, w2_ref, b2_ref, w3_ref, b3_ref, fr_ref, dec_ref,
                   a_ref, o_ref, *, L):
    fr = fr_ref[...]
    h = jnp.sin(fr * (jnp.dot(z_ref[...], w1_ref[...], precision=HIGHEST,
                              preferred_element_type=F32) + b1_ref[...]))
    h = jnp.sin(fr * (jnp.dot(h, w2_ref[...], precision=HIGHEST,
                              preferred_element_type=F32) + b2_ref[...]))
    h = jnp.dot(h, w3_ref[...], precision=HIGHEST, preferred_element_type=F32) + b3_ref[...]
    dec = dec_ref[...]
    row0 = lax.broadcasted_iota(jnp.int32, (L, 1), 0) == 0
    fwd = h[:, :HY_CH] * dec
    bwd = jnp.where(row0, 0.0, h[:, HY_CH:] * dec)
    gp = (fwd + bwd).astype(BF16)
    gm = (fwd - bwd).astype(BF16)
    top = _dot(a_ref[0:L, :], gp)
    bot = _dot(a_ref[L:2 * L, :], gm)
    nyq = _dot(a_ref[L:L + 16, :], gp)[0:1]
    o_ref[0, 0:L, :] = top
    o_ref[0, L:2 * L, :] = jnp.where(row0, nyq, bot)


def _filter_spectrum(L, a_fwd, w1, b1, w2, b2, w3, b3, freq):
    z, decay = _filter_consts(L)
    full = lambda shape: pl.BlockSpec(shape, lambda o: (0,) * len(shape))
    return pl.pallas_call(
        functools.partial(_filter_kernel, L=L),
        grid=(2,),
        in_specs=[
            full((L, FILT_EMB_PAD)),
            full((FILT_EMB_PAD, FILT_HID)),
            full((1, FILT_HID)),
            full((FILT_HID, FILT_HID)),
            full((1, FILT_HID)),
            pl.BlockSpec((FILT_HID, 2 * HY_CH), lambda o: (0, o)),
            pl.BlockSpec((1, 2 * HY_CH), lambda o: (0, o)),
            full((1, FILT_HID)),
            full((L, HY_CH)),
            full((2 * L, L)),
        ],
        out_specs=pl.BlockSpec((1, 2 * L, HY_CH), lambda o: (o, 0, 0)),
        out_shape=jax.ShapeDtypeStruct((2, 2 * L, HY_CH), F32),
        compiler_params=_cparams(("arbitrary",)),
        name=f"filter_{L}",
    )(jnp.asarray(z), jnp.pad(w1, ((0, FILT_EMB_PAD - w1.shape[0]), (0, 0))), b1[None], w2, b2[None],
      w3, b3[None], freq[None], jnp.asarray(decay), a_fwd)


def _store_heads_major(x, dst_ref, L):
    for s in range(x.shape[0] // L):
        for h in range(N_HEADS):
            dst_ref[s, 0, h] = x[s * L:(s + 1) * L, h * HEAD_DIM:(h + 1) * HEAD_DIM]


def _proj_kernel(x_ref, mods_ref, g_ref, w_ref, qkv_ref, hy_ref, *state, mod_base, blocks_per_mod, L):
    r = mod_base + pl.program_id(0) // blocks_per_mod
    sh1 = mods_ref[0, pl.ds(r, 1), :]
    sc1 = mods_ref[1, pl.ds(r, 1), :]
    h = (_rms(x_ref[...]) * g_ref[...] * (1.0 + sc1) + sh1).astype(BF16)
    q = _dot(h, w_ref[:, 0:ATTN_W])
    qkv_ref[:, 0:ATTN_W] = (q * HEAD_DIM ** -0.5).astype(BF16)
    k = _dot(h, w_ref[:, ATTN_W:2 * ATTN_W])
    qkv_ref[:, ATTN_W:2 * ATTN_W] = k.astype(BF16)
    v = _dot(h, w_ref[:, 2 * ATTN_W:3 * ATTN_W])
    qkv_ref[:, 2 * ATTN_W:3 * ATTN_W] = v.astype(BF16)
    if state:
        ks_ref, vs_ref = state
        _store_heads_major(k, ks_ref, L)
        _store_heads_major(v, vs_ref, L)
    hy_ref[...] = _dot(h, w_ref[:, 3 * ATTN_W:])


def _proj(x, mods, g, w_in, *, mod_base, blocks_per_mod, L, emit_state):
    n = x.shape[0]
    row = lambda w: pl.BlockSpec((PROJ_ROWS, w), lambda i: (i, 0))
    out_specs = [row(3 * ATTN_W), row(3 * HY_CH)]
    out_shape = [jax.ShapeDtypeStruct((n, 3 * ATTN_W), BF16), jax.ShapeDtypeStruct((n, 3 * HY_CH), F32)]
    if emit_state:
        seqs = PROJ_ROWS // L
        state = pl.BlockSpec((seqs, 1, N_HEADS, L, HEAD_DIM), lambda i: (i, 0, 0, 0, 0))
        out_specs += [state, state]
        out_shape += [jax.ShapeDtypeStruct((n // L, 1, N_HEADS, L, HEAD_DIM), F32)] * 2
    return pl.pallas_call(
        functools.partial(_proj_kernel, mod_base=mod_base, blocks_per_mod=blocks_per_mod, L=L),
        grid=(n // PROJ_ROWS,),
        in_specs=[
            row(D_MODEL),
            pl.BlockSpec((6, 8, D_MODEL), lambda i: (0, 0, 0)),
            pl.BlockSpec((1, D_MODEL), lambda i: (0, 0)),
            pl.BlockSpec((D_MODEL, IN_COLS), lambda i: (0, 0)),
        ],
        out_specs=out_specs,
        out_shape=out_shape,
        compiler_params=_cparams(("arbitrary",)),
        name="proj_ctx" if emit_state else "proj_lat",
    )(x, mods, g, w_in)


def _softmax_pv(scores, values):
    mx = functools.reduce(jnp.maximum, [jnp.max(s, axis=-1, keepdims=True) for s in scores])
    return functools.reduce(jnp.add, [_dot(jnp.exp(s - mx).astype(BF16), v) for s, v in zip(scores, values)])


def _normalise_heads(o0, o1, low):
    den = pltpu.roll(jnp.where(low, o1, o0), HEAD_DIM, 1)
    return jnp.where(low, o0, o1) * (1.0 / den)


def _split_heads(k, v, masks):
    ks = [jnp.where(m, k, jnp.zeros_like(k)) for m in masks]
    vs = [jnp.where(m, v, jnp.ones_like(v)) for m in masks]
    return ks, vs


def _ctx_attn_kernel(qkv_ref, o_ref, *, n_seq, L):
    low = lax.broadcasted_iota(jnp.int32, (1, LANES), 1) < HEAD_DIM
    masks = (low, jnp.logical_not(low))

    def seq_body(s, carry):
        rows = pl.ds(pl.multiple_of(s * L, L), L)
        for p in range(ATTN_W // LANES):
            cols = lambda part: slice(part * ATTN_W + p * LANES, part * ATTN_W + (p + 1) * LANES)
            qb = qkv_ref[rows, cols(0)]
            kb = qkv_ref[rows, cols(1)]
            vb = qkv_ref[rows, cols(2)]
            outs = []
            for m in masks:
                s_ = _dot_t(qb, jnp.where(m, kb, jnp.zeros_like(kb)))
                p_ = jnp.exp(s_ - jnp.max(s_, axis=-1, keepdims=True))
                outs.append(_dot(p_.astype(BF16), vb) * (1.0 / jnp.sum(p_, axis=-1, keepdims=True)))
            o_ref[rows, p * LANES:(p + 1) * LANES] = jnp.where(low, outs[0], outs[1])
        return carry

    lax.fori_loop(0, n_seq, seq_body, 0)


def _ctx_attn(qkv, L):
    n = qkv.shape[0]
    return pl.pallas_call(
        functools.partial(_ctx_attn_kernel, n_seq=ROW_BLOCK // L, L=L),
        grid=(n // ROW_BLOCK,),
        in_specs=[pl.BlockSpec((ROW_BLOCK, 3 * ATTN_W), lambda i: (i, 0))],
        out_specs=pl.BlockSpec((ROW_BLOCK, ATTN_W), lambda i: (i, 0)),
        out_shape=jax.ShapeDtypeStruct((n, ATTN_W), F32),
        compiler_params=_cparams(("arbitrary",)),
        name="ctx_attn",
    )(qkv)


N_DR = 2 * WIN_ROWS - 1
N_DC = 2 * WIN_COLS - 1


def _rpb_expand_table():
    d = np.clip(np.arange(LANES) - GRID_W, -(WIN_COLS - 1), WIN_COLS - 1) + (WIN_COLS - 1)
    t = np.zeros((32, LANES), np.float32)
    t[d, np.arange(LANES)] = 1.0
    return t


def _bias_tile_index(n_rows):
    n_blk = n_rows // Q_ROWS
    idx = np.full((n_blk, Q_ROWS, KEY_ROWS), N_DR, np.int32)
    for i in range(n_blk):
        ks = min(max(Q_ROWS * i - WIN_ROWS // 2, 0), n_rows - KEY_ROWS)
        for a in range(Q_ROWS):
            r = Q_ROWS * i + a
            rs = min(max(r - WIN_ROWS // 2, 0), n_rows - WIN_ROWS)
            for j in range(KEY_ROWS):
                if rs <= ks + j < rs + WIN_ROWS:
                    idx[i, a, j] = ks + j - r + (WIN_ROWS - 1)
            assert (idx[i, a] != N_DR).sum() == WIN_ROWS, "key window must cover the whole band"
    return idx


def _build_bias(rpb_ref, ext_ref, tw_ref, bias_ref, n_blk):
    shape = (GRID_W, LANES)
    lane = lax.broadcasted_iota(jnp.int32, shape, 1)
    qc = lax.broadcasted_iota(jnp.int32, shape, 0)
    kc = lane & (GRID_W - 1)
    low = lane < GRID_W
    col_start = jnp.clip(qc - WIN_COLS // 2, 0, GRID_W - WIN_COLS)
    col_in = (kc >= col_start) & (kc < col_start + WIN_COLS)
    idx = _bias_tile_index(Q_ROWS * n_blk)
    for e in range(2):
        g = jnp.dot(rpb_ref[e], ext_ref[...], precision=HIGHEST, preferred_element_type=F32)
        for dr in range(N_DR):
            t = pltpu.roll(jnp.broadcast_to(g[dr:dr + 1, :], shape), GRID_W, 1, stride=1, stride_axis=0)
            t = jnp.where(low, t, pltpu.roll(t, GRID_W, 1))
            tw_ref[e, dr] = jnp.where(col_in, t, NEG)
        tw_ref[e, N_DR] = jnp.full(shape, NEG, F32)
        for i in range(n_blk):
            for a in range(Q_ROWS):
                for jt in range(KEY_ROWS // 2):
                    tile = jnp.where(low, tw_ref[e, int(idx[i, a, 2 * jt])], tw_ref[e, int(idx[i, a, 2 * jt + 1])])
                    bias_ref[e, i, a * GRID_W:(a + 1) * GRID_W, jt * LANES:(jt + 1) * LANES] = tile


def _lat_attn_kernel(q_ref, k_ref, v_ref, kc_ref, vc_ref, rpb_ref, ext_ref, o_ref, km_ref, vm_ref, tw_ref,
                     bias_ref, *, n_blk):
    @pl.when(pl.program_id(1) == 0)
    def _():
        _build_bias(rpb_ref, ext_ref, tw_ref, bias_ref, n_blk)

    low = lax.broadcasted_iota(jnp.int32, (1, LANES), 1) < HEAD_DIM
    masks = (low, jnp.logical_not(low))
    ks, vs = _split_heads(k_ref[...], v_ref[...], masks)
    for e in range(2):
        km_ref[e] = ks[e]
        vm_ref[e] = vs[e]
    kce, vce = _split_heads(kc_ref[0], vc_ref[0], masks)
    nq = Q_ROWS * GRID_W
    nk = KEY_ROWS * GRID_W
    for i in range(n_blk):
        qrows = slice(i * nq, (i + 1) * nq)
        k0 = min(max(Q_ROWS * i - WIN_ROWS // 2, 0), Q_ROWS * n_blk - KEY_ROWS) * GRID_W
        krows = slice(k0, k0 + nk)
        qb = q_ref[qrows, :]
        outs = []
        for e in range(2):
            s_loc = _dot_t(qb, km_ref[e, krows, :]) + bias_ref[e, i]
            s_ctx = _dot_t(qb, kce[e])
            outs.append(_softmax_pv([s_loc, s_ctx], [vm_ref[e, krows, :], vce[e]]))
        o_ref[qrows, :] = _normalise_heads(outs[0], outs[1], low)


def _lat_attn(qkv, kc, vc, rpb, batch, L):
    n_blk = L // (Q_ROWS * GRID_W)
    n_pairs = ATTN_W // LANES
    col = lambda part: pl.BlockSpec((L, LANES), lambda p, b: (b, part * n_pairs + p))
    cache = pl.BlockSpec((1, kc.shape[1], LANES), lambda p, b: (b, 0, p))
    rpb_pad = jnp.pad(rpb, ((0, 0), (0, 16 - N_DR), (0, 32 - N_DC)))
    return pl.pallas_call(
        functools.partial(_lat_attn_kernel, n_blk=n_blk),
        grid=(n_pairs, batch),
        in_specs=[col(0), col(1), col(2), cache, cache,
                  pl.BlockSpec((2, 16, 32), lambda p, b: (p, 0, 0)),
                  pl.BlockSpec((32, LANES), lambda p, b: (0, 0))],
        out_specs=pl.BlockSpec((L, LANES), lambda p, b: (b, p)),
        out_shape=jax.ShapeDtypeStruct((batch * L, ATTN_W), F32),
        scratch_shapes=[pltpu.VMEM((2, L, LANES), BF16), pltpu.VMEM((2, L, LANES), BF16),
                        pltpu.VMEM((2, 16, GRID_W, LANES), F32),
                        pltpu.VMEM((2, n_blk, Q_ROWS * GRID_W, KEY_ROWS * GRID_W), F32)],
        compiler_params=_cparams(("arbitrary", "arbitrary")),
        name="lat_attn",
    )(qkv, qkv, qkv, kc, vc, rpb_pad, jnp.asarray(_rpb_expand_table()))


SUBLANES = 8


def _row_neighbours(x, L):
    n, c = x.shape
    t, tl = n // SUBLANES, L // SUBLANES
    x3 = x.reshape(t, SUBLANES, c)
    sub = lax.broadcasted_iota(jnp.int32, (1, SUBLANES, 1), 1)
    zero = jnp.zeros((1, SUBLANES, c), x.dtype)
    down = pltpu.roll(x3, 1, 1)
    up = pltpu.roll(x3, SUBLANES - 1, 1)
    down_prev = jnp.concatenate([p for s in range(0, t, tl) for p in (zero, down[s:s + tl - 1])], axis=0)
    up_next = jnp.concatenate([p for s in range(0, t, tl) for p in (up[s + 1:s + tl], zero)], axis=0)
    above = jnp.where(sub == 0, down_prev, down).reshape(n, c)
    below = jnp.where(sub == SUBLANES - 1, up_next, up).reshape(n, c)
    return above, below


def _dwconv3(x, w_ref, b_ref, L):
    above, below = _row_neighbours(x, L)
    return above * w_ref[0:1, :] + x * w_ref[1:2, :] + below * w_ref[2:3, :] + b_ref[...]


def _seq_edges(n, L):
    pos = lax.broadcasted_iota(jnp.int32, (n, 1), 0) & (L - 1)
    return pos == 0, pos == L - 1


def _hyena_kernel(h1_ref, h2_ref, hv_ref, w1_ref, w2_ref, wv_ref, b1_ref, b2_ref, bv_ref,
                  kf_ref, fb_ref, a_ref, ai_ref, o_ref, x1_s, x2_s, v_s, *, L):
    n = h1_ref.shape[0]
    x1_s[...] = _dwconv3(h1_ref[...], w1_ref, b1_ref, L)
    x2_s[...] = _dwconv3(h2_ref[...], w2_ref, b2_ref, L)
    v_s[...] = _dwconv3(hv_ref[...], wv_ref, bv_ref, L)
    row0 = lax.broadcasted_iota(jnp.int32, (L, 1), 0) == 0

    def long_conv(u, order):
        ub = u.astype(BF16)
        ur = _dot(a_ref[0:L, :], ub)
        ui = _dot(a_ref[L:2 * L, :], ub)
        kr = kf_ref[order, 0:L, :]
        ki = kf_ref[order, L:2 * L, :]
        yr = jnp.where(row0, ur * kr, ur * kr - ui * ki)
        yi = jnp.where(row0, ui * ki, ur * ki + ui * kr)
        y = _dot(ai_ref[:, 0:L], yr.astype(BF16)) + _dot(ai_ref[:, L:2 * L], yi.astype(BF16))
        return y + u * fb_ref[order:order + 1, :]

    for s in range(n // L):
        rows = slice(s * L, (s + 1) * L)
        z = x1_s[rows, :] * long_conv(v_s[rows, :], 0)
        o_ref[rows, :] = x2_s[rows, :] * long_conv(z, 1)


def _hyena(hy, conv_w, conv_b, kf, filt_bias, a_fwd, a_inv, L):
    n = hy.shape[0]
    part = lambda k, rows: pl.BlockSpec((rows, HY_CH), lambda i: (i if rows == ROW_BLOCK else 0, k))
    once = lambda shape: pl.BlockSpec(shape, lambda i: (0,) * len(shape), pipeline_mode=pl.Buffered(1))
    return pl.pallas_call(
        functools.partial(_hyena_kernel, L=L),
        grid=(n // ROW_BLOCK,),
        in_specs=[part(0, ROW_BLOCK), part(1, ROW_BLOCK), part(2, ROW_BLOCK),
                  part(0, 3), part(1, 3), part(2, 3),
                  part(0, 1), part(1, 1), part(2, 1),
                  once((2, 2 * L, HY_CH)), once((2, HY_CH)), once((2 * L, L)), once((L, 2 * L))],
        out_specs=pl.BlockSpec((ROW_BLOCK, HY_CH), lambda i: (i, 0)),
        out_shape=jax.ShapeDtypeStruct((n, HY_CH), F32),
        scratch_shapes=[pltpu.VMEM((ROW_BLOCK, HY_CH), F32)] * 3,
        compiler_params=_cparams(("arbitrary",)),
        name=f"hyena_{L}",
    )(hy, hy, hy, conv_w, conv_w, conv_w, conv_b, conv_b, conv_b, kf, filt_bias, a_fwd, a_inv)


def _tail_kernel(*refs, L, mod_base, blocks_per_mod, halo):
    rows_refs, rest = refs[:9 if halo else 3], refs[9 if halo else 3:]
    (mods_ref, gg_ref, wo_ref, n2_ref, fg_ref, wg_ref, wu_ref, cw_ref, cb_ref, wd_ref, o_ref,
     h2_s, acc_s, ma_s, mb_s) = rest
    rows = o_ref.shape[0]
    i = pl.program_id(0)
    r = mod_base + i // blocks_per_mod
    mod = lambda k: mods_ref[k, pl.ds(r, 1), :]

    if halo:
        x_ref, a_ref, hy_ref, xp_ref, ap_ref, hp_ref, xn_ref, an_ref, hn_ref = rows_refs
        cat = lambda p, m, n: jnp.concatenate([p[...], m[...], n[...]], axis=0)
        x, a, hy = cat(xp_ref, x_ref, xn_ref), cat(ap_ref, a_ref, an_ref), cat(hp_ref, hy_ref, hn_ref)
    else:
        x, a, hy = (ref[...] for ref in rows_refs)
    gg = gg_ref[...]
    merged = jnp.concatenate([_rms(a) * gg[:, :ATTN_W], _rms(hy) * gg[:, ATTN_W:]], axis=-1).astype(BF16)
    x1 = x + mod(2) * _dot(merged, wo_ref[...])
    h2_s[...] = (_rms(x1) * n2_ref[...] * (1.0 + mod(4)) + mod(3)).astype(BF16)
    o_ref[...] = x1[halo:halo + rows]

    pos = (lax.broadcasted_iota(jnp.int32, (rows, 1), 0) + i * rows) & (L - 1)
    first, last = pos == 0, pos == L - 1
    ext = rows + 2 * halo
    base = TAIL_PAD + halo

    def up_proj(j, dst):
        dst[0, TAIL_PAD:TAIL_PAD + ext, :] = _dot(h2_s[...], wg_ref[j])
        dst[1, TAIL_PAD:TAIL_PAD + ext, :] = _dot(h2_s[...], wu_ref[j])

    def conv(src, e, j):
        w = cw_ref.at[e, j]
        prev = jnp.where(first, 0.0, src[e, base - 1:base - 1 + rows, :])
        nxt = jnp.where(last, 0.0, src[e, base + 1:base + 1 + rows, :])
        return prev * w[0:1, :] + src[e, base:base + rows, :] * w[1:2, :] + nxt * w[2:3, :] + cb_ref[e, j]

    def down_proj(j, src):
        gate = conv(src, 0, j)
        act = gate / (1.0 + jnp.exp(-gate)) * conv(src, 1, j)
        return _dot(act.astype(BF16), wd_ref[j])

    for buf in (ma_s, mb_s):
        for e in range(2):
            buf[e, 0:TAIL_PAD, :] = jnp.zeros((TAIL_PAD, FF_TILE), F32)
            buf[e, TAIL_PAD + ext:, :] = jnp.zeros((TAIL_PAD, FF_TILE), F32)
    up_proj(0, ma_s)
    acc_s[...] = jnp.zeros_like(acc_s)

    def two_tiles(k, carry):
        j = 2 * k
        up_proj(j + 1, mb_s)
        acc_s[...] += down_proj(j, ma_s)
        up_proj(j + 2, ma_s)
        acc_s[...] += down_proj(j + 1, mb_s)
        return carry

    lax.fori_loop(0, (N_FF_TILES - 1) // 2, two_tiles, 0)
    x2 = o_ref[...] + mod(5) * (acc_s[...] + down_proj(N_FF_TILES - 1, ma_s))
    o_ref[...] = _rms(x2) * fg_ref[...]


def _tail(x, a, hyo, mods, grp_g, w_out, n2_g, final_g, w_gate, w_upp, conv_w, conv_b, w_down, *, L, mod_base,
          blocks_per_mod):
    n = x.shape[0]
    halo = TAIL_HALO if L > TAIL_ROWS else 0
    row = lambda w: pl.BlockSpec((TAIL_ROWS, w), lambda i: (i, 0))
    per_halo = TAIL_ROWS // TAIL_HALO
    prev = lambda w: pl.BlockSpec((TAIL_HALO, w), lambda i: (jnp.maximum(i * per_halo - 1, 0), 0))
    nxt = lambda w: pl.BlockSpec((TAIL_HALO, w), lambda i: (jnp.minimum((i + 1) * per_halo, n // TAIL_HALO - 1), 0))
    once = lambda shape: pl.BlockSpec(shape, lambda i: (0,) * len(shape), pipeline_mode=pl.Buffered(1))
    widths = (D_MODEL, ATTN_W, HY_CH)
    row_specs = [row(w) for w in widths]
    row_args = [x, a, hyo]
    if halo:
        row_specs += [prev(w) for w in widths] + [nxt(w) for w in widths]
        row_args += [x, a, hyo, x, a, hyo]
    ext = TAIL_ROWS + 2 * halo
    return pl.pallas_call(
        functools.partial(_tail_kernel, L=L, mod_base=mod_base, blocks_per_mod=blocks_per_mod, halo=halo),
        grid=(n // TAIL_ROWS,),
        in_specs=row_specs + [
            once((6, 8, D_MODEL)), once((1, D_MODEL)), once((D_MODEL, D_MODEL)), once((1, D_MODEL)),
            once((1, D_MODEL)),
            once((N_FF_TILES, D_MODEL, FF_TILE)), once((N_FF_TILES, D_MODEL, FF_TILE)),
            once((2, N_FF_TILES, 3, FF_TILE)), once((2, N_FF_TILES, 1, FF_TILE)),
            once((N_FF_TILES, FF_TILE, D_MODEL))],
        out_specs=row(D_MODEL),
        out_shape=jax.ShapeDtypeStruct((n, D_MODEL), F32),
        scratch_shapes=[pltpu.VMEM((ext, D_MODEL), BF16), pltpu.VMEM((TAIL_ROWS, D_MODEL), F32),
                        pltpu.VMEM((2, ext + 2 * TAIL_PAD, FF_TILE), F32),
                        pltpu.VMEM((2, ext + 2 * TAIL_PAD, FF_TILE), F32)],
        compiler_params=_cparams(("arbitrary",)),
        name=f"tail_{L}",
    )(*row_args, mods, grp_g, w_out, n2_g, final_g, w_gate, w_upp, conv_w, conv_b, w_down)


def kernel(x_prompt, x_sample, cache_ctx_k, cache_ctx_v, c, c_ctx, w_ada, b_ada, norm1_g, w_in, rpb,
           hy_conv_w, hy_conv_b, filt_w1, filt_b1, filt_w2, filt_b2, filt_w3, filt_b3, filt_freq,
           filt_bias, grp_norm_g, w_out, norm2_g, w_up, ffn_conv_w, ffn_conv_b, w_down, final_g):
    assert w_ada.shape[0] == 1, "single-layer trunk"
    bc, lc, _ = x_prompt.shape
    bl, ll, _ = x_sample.shape
    past = cache_ctx_k.shape[3]

    cvec = jnp.concatenate([c_ctx[None], c, jnp.zeros((8 - 1 - bl, D_MODEL), F32)], axis=0)
    mods = _mods(cvec, w_ada[0], b_ada[0])

    w_in_b = w_in[0].astype(BF16)
    w_out_b = w_out[0].astype(BF16)
    w_up_t = w_up[0].astype(BF16).reshape(D_MODEL, 2, N_FF_TILES, FF_TILE).transpose(1, 2, 0, 3)
    w_down_t = w_down[0].astype(BF16).reshape(N_FF_TILES, FF_TILE, D_MODEL)
    conv_w_t = ffn_conv_w[0].reshape(3, 2, N_FF_TILES, FF_TILE).transpose(1, 2, 0, 3)
    conv_b_t = ffn_conv_b[0].reshape(2, N_FF_TILES, 1, FF_TILE)
    g1 = norm1_g[0][None]
    shared_tail = (grp_norm_g[0][None], w_out_b, norm2_g[0][None], final_g[None], w_up_t[0], w_up_t[1],
                   conv_w_t, conv_b_t, w_down_t)
    filt = (filt_w1[0], filt_b1[0], filt_w2[0], filt_b2[0], filt_w3[0], filt_b3[0], filt_freq[0])

    def tables(L):
        fwd, inv = _dft_tables(L)
        return jnp.asarray(fwd).astype(BF16), jnp.asarray(inv).astype(BF16)

    xc = x_prompt.reshape(bc * lc, D_MODEL)
    a_fwd, a_inv = tables(lc)
    kf = _filter_spectrum(lc, a_fwd, *filt)
    qkv, hy, state_k, state_v = _proj(xc, mods, g1, w_in_b, mod_base=0, blocks_per_mod=bc * lc // PROJ_ROWS,
                                      L=lc, emit_state=True)
    att = _ctx_attn(qkv, lc)
    hyo = _hyena(hy, hy_conv_w[0], hy_conv_b[0][None], kf, filt_bias[0], a_fwd, a_inv, lc)
    y_prompt = _tail(xc, att, hyo, mods, *shared_tail, L=lc, mod_base=0,
                     blocks_per_mod=bc * lc // TAIL_ROWS).reshape(bc, lc, D_MODEL)

    xs = x_sample.reshape(bl * ll, D_MODEL)
    a_fwd, a_inv = tables(ll)
    kf = _filter_spectrum(ll, a_fwd, *filt)
    qkv, hy = _proj(xs, mods, g1, w_in_b, mod_base=1, blocks_per_mod=ll // PROJ_ROWS, L=ll, emit_state=False)
    lanes_major = lambda t: t[:, 0].transpose(0, 2, 1, 3).reshape(bl, past, ATTN_W).astype(BF16)
    att = _lat_attn(qkv, lanes_major(cache_ctx_k), lanes_major(cache_ctx_v), rpb[0], bl, ll)
    hyo = _hyena(hy, hy_conv_w[0], hy_conv_b[0][None], kf, filt_bias[0], a_fwd, a_inv, ll)
    y_sample = _tail(xs, att, hyo, mods, *shared_tail, L=ll, mod_base=1,
                     blocks_per_mod=ll // TAIL_ROWS).reshape(bl, ll, D_MODEL)

    return (y_prompt, y_sample, state_k, state_v)
```

```python
import functools
import math

import numpy as np
import jax
import jax.numpy as jnp
from jax import lax
from jax.experimental import pallas as pl
from jax.experimental.pallas import tpu as pltpu

F32 = jnp.float32
BF16 = jnp.bfloat16
HIGHEST = lax.Precision.HIGHEST

D_MODEL = 1024
N_HEADS = 8
HEAD_DIM = 64
ATTN_W = N_HEADS * HEAD_DIM
HY_CH = D_MODEL - ATTN_W
IN_COLS = 3 * ATTN_W + 3 * HY_CH
D_FF = 2816
GRID_W = 64
WIN_ROWS = 8
WIN_COLS = 16
FILT_FREQS = 8
FILT_HID = 64
FILT_EMB_PAD = 32
DECAY_TARGET = 1e-2
MAX_DECAY = math.log(DECAY_TARGET) / 0.3
MIN_DECAY = math.log(DECAY_TARGET) / 1.5
EPS = 1e-6
NEG = -1e30

LANES = 128
ROW_BLOCK = 1024
PROJ_ROWS = 512
TAIL_ROWS = 512
TAIL_HALO = 16
TAIL_PAD = 8
FF_TILE = 256
N_FF_TILES = D_FF // FF_TILE
Q_ROWS = 4
KEY_ROWS = 12
VMEM_LIMIT = 56 * 1024 * 1024


def _cparams(sem):
    return pltpu.CompilerParams(dimension_semantics=sem, vmem_limit_bytes=VMEM_LIMIT)


def _rms(x):
    return x * lax.rsqrt(jnp.mean(x * x, axis=-1, keepdims=True) + EPS)


def _dot(a, b):
    return jnp.dot(a, b, preferred_element_type=F32)


def _dot3(a, b):
    ah = a.astype(BF16)
    bh = b.astype(BF16)
    al = (a - ah.astype(F32)).astype(BF16)
    bl = (b - bh.astype(F32)).astype(BF16)
    return _dot(ah, bh) + _dot(al, bh) + _dot(ah, bl)


def _dot_t(a, b):
    return lax.dot_general(a, b, (((1,), (1,)), ((), ())), preferred_element_type=F32)


def _mods_kernel(c_ref, w_ref, b_ref, o_ref):
    cv = c_ref[...]
    s = cv / (1.0 + jnp.exp(-cv))
    o_ref[0] = _dot3(s, w_ref[...]) + b_ref[0]


def _mods(cvec, w_ada, b_ada):
    return pl.pallas_call(
        _mods_kernel,
        grid=(6,),
        in_specs=[
            pl.BlockSpec((8, D_MODEL), lambda j: (0, 0)),
            pl.BlockSpec((D_MODEL, D_MODEL), lambda j: (0, j)),
            pl.BlockSpec((1, 1, D_MODEL), lambda j: (j, 0, 0)),
        ],
        out_specs=pl.BlockSpec((1, 8, D_MODEL), lambda j: (j, 0, 0)),
        out_shape=jax.ShapeDtypeStruct((6, 8, D_MODEL), F32),
        compiler_params=_cparams(("arbitrary",)),
        name="mods",
    )(cvec, w_ada, b_ada.reshape(6, 1, D_MODEL))


def _dft_tables(L):
    f = np.arange(L, dtype=np.int64)[:, None]
    t = np.arange(L, dtype=np.int64)[None, :]
    ang = np.pi * ((f * t) % (2 * L)).astype(np.float64) / L
    top = np.cos(ang)
    bot = -np.sin(ang)
    bot[0, :] = np.where(np.arange(L) % 2 == 0, 1.0, -1.0)
    fwd = np.concatenate([top, bot], axis=0)
    w = np.full((2 * L,), 1.0 / L)
    w[0] = w[L] = 0.5 / L
    inv = (fwd * w[:, None]).T
    return fwd.astype(np.float32), inv.astype(np.float32)


def _filter_consts(L):
    t = np.arange(L, dtype=np.float64) / L
    fr = np.arange(1, FILT_FREQS + 1, dtype=np.float64)
    ang = 2.0 * math.pi * t[:, None] * fr[None, :]
    z = np.zeros((L, FILT_EMB_PAD), np.float64)
    z[:, 0] = t
    z[:, 1:1 + FILT_FREQS] = np.cos(ang)
    z[:, 1 + FILT_FREQS:1 + 2 * FILT_FREQS] = np.sin(ang)
    deltas = np.abs(np.linspace(MIN_DECAY, MAX_DECAY, HY_CH))
    decay = np.exp(-t[:, None] * deltas[None, :])
    return z.astype(np.float32), decay.astype(np.float32)


def _filter_kernel(z_ref, w1_ref, b1_ref, w2_ref, b2_ref, w3_ref, b3_ref, fr_ref, dec_ref,
                   a_ref, o_ref, *, L):
    fr = fr_ref[...]
    h = jnp.sin(fr * (jnp.dot(z_ref[...], w1_ref[...], precision=HIGHEST,
                              preferred_element_type=F32) + b1_ref[...]))
    h = jnp.sin(fr * (jnp.dot(h, w2_ref[...], precision=HIGHEST,
                              preferred_element_type=F32) + b2_ref[...]))
    h = _dot3(h, w3_ref[...]) + b3_ref[...]
    dec = dec_ref[...]
    row0 = lax.broadcasted_iota(jnp.int32, (L, 1), 0) == 0
    fwd = h[:, :HY_CH] * dec
    bwd = jnp.where(row0, 0.0, h[:, HY_CH:] * dec)
    gp = (fwd + bwd).astype(BF16)
    gm = (fwd - bwd).astype(BF16)
    top = _dot(a_ref[0:L, :], gp)
    bot = _dot(a_ref[L:2 * L, :], gm)
    nyq = _dot(a_ref[L:L + 16, :], gp)[0:1]
    o_ref[0, 0:L, :] = top
    o_ref[0, L:2 * L, :] = jnp.where(row0, nyq, bot)


def _filter_spectrum(L, a_fwd, w1, b1, w2, b2, w3, b3, freq):
    z, decay = _filter_consts(L)
    full = lambda shape: pl.BlockSpec(shape, lambda o: (0,) * len(shape))
    return pl.pallas_call(
        functools.partial(_filter_kernel, L=L),
        grid=(2,),
        in_specs=[
            full((L, FILT_EMB_PAD)),
            full((FILT_EMB_PAD, FILT_HID)),
            full((1, FILT_HID)),
            full((FILT_HID, FILT_HID)),
            full((1, FILT_HID)),
            pl.BlockSpec((FILT_HID, 2 * HY_CH), lambda o: (0, o)),
            pl.BlockSpec((1, 2 * HY_CH), lambda o: (0, o)),
            full((1, FILT_HID)),
            full((L, HY_CH)),
            full((2 * L, L)),
        ],
        out_specs=pl.BlockSpec((1, 2 * L, HY_CH), lambda o: (o, 0, 0)),
        out_shape=jax.ShapeDtypeStruct((2, 2 * L, HY_CH), F32),
        compiler_params=_cparams(("arbitrary",)),
        name=f"filter_{L}",
    )(jnp.asarray(z), jnp.pad(w1, ((0, FILT_EMB_PAD - w1.shape[0]), (0, 0))), b1[None], w2, b2[None],
      w3, b3[None], freq[None], jnp.asarray(decay), a_fwd)


def _store_heads_major(x, dst_ref, L):
    for s in range(x.shape[0] // L):
        for h in range(N_HEADS):
            dst_ref[s, 0, h] = x[s * L:(s + 1) * L, h * HEAD_DIM:(h + 1) * HEAD_DIM]


def _proj_kernel(x_ref, mods_ref, g_ref, w_ref, qkv_ref, hy_ref, *state, mod_base, blocks_per_mod, L):
    r = mod_base + pl.program_id(0) // blocks_per_mod
    sh1 = mods_ref[0, pl.ds(r, 1), :]
    sc1 = mods_ref[1, pl.ds(r, 1), :]
    h = (_rms(x_ref[...]) * g_ref[...] * (1.0 + sc1) + sh1).astype(BF16)
    q = _dot(h, w_ref[:, 0:ATTN_W])
    qkv_ref[:, 0:ATTN_W] = (q * HEAD_DIM ** -0.5).astype(BF16)
    k = _dot(h, w_ref[:, ATTN_W:2 * ATTN_W])
    qkv_ref[:, ATTN_W:2 * ATTN_W] = k.astype(BF16)
    v = _dot(h, w_ref[:, 2 * ATTN_W:3 * ATTN_W])
    qkv_ref[:, 2 * ATTN_W:3 * ATTN_W] = v.astype(BF16)
    if state:
        ks_ref, vs_ref = state
        _store_heads_major(k, ks_ref, L)
        _store_heads_major(v, vs_ref, L)
    hy_ref[...] = _dot(h, w_ref[:, 3 * ATTN_W:])


def _proj(x, mods, g, w_in, *, mod_base, blocks_per_mod, L, emit_state):
    n = x.shape[0]
    row = lambda w: pl.BlockSpec((PROJ_ROWS, w), lambda i: (i, 0))
    out_specs = [row(3 * ATTN_W), row(3 * HY_CH)]
    out_shape = [jax.ShapeDtypeStruct((n, 3 * ATTN_W), BF16), jax.ShapeDtypeStruct((n, 3 * HY_CH), F32)]
    if emit_state:
        seqs = PROJ_ROWS // L
        state = pl.BlockSpec((seqs, 1, N_HEADS, L, HEAD_DIM), lambda i: (i, 0, 0, 0, 0))
        out_specs += [state, state]
        out_shape += [jax.ShapeDtypeStruct((n // L, 1, N_HEADS, L, HEAD_DIM), F32)] * 2
    return pl.pallas_call(
        functools.partial(_proj_kernel, mod_base=mod_base, blocks_per_mod=blocks_per_mod, L=L),
        grid=(n // PROJ_ROWS,),
        in_specs=[
            row(D_MODEL),
            pl.BlockSpec((6, 8, D_MODEL), lambda i: (0, 0, 0)),
            pl.BlockSpec((1, D_MODEL), lambda i: (0, 0)),
            pl.BlockSpec((D_MODEL, IN_COLS), lambda i: (0, 0)),
        ],
        out_specs=out_specs,
        out_shape=out_shape,
        compiler_params=_cparams(("arbitrary",)),
        name="proj_ctx" if emit_state else "proj_lat",
    )(x, mods, g, w_in)


def _softmax_pv(scores, values):
    mx = functools.reduce(jnp.maximum, [jnp.max(s, axis=-1, keepdims=True) for s in scores])
    return functools.reduce(jnp.add, [_dot(jnp.exp(s - mx).astype(BF16), v) for s, v in zip(scores, values)])


def _normalise_heads(o0, o1, low):
    den = pltpu.roll(jnp.where(low, o1, o0), HEAD_DIM, 1)
    return jnp.where(low, o0, o1) * (1.0 / den)


def _split_heads(k, v, masks):
    ks = [jnp.where(m, k, jnp.zeros_like(k)) for m in masks]
    vs = [jnp.where(m, v, jnp.ones_like(v)) for m in masks]
    return ks, vs


def _ctx_attn_kernel(qkv_ref, o_ref, *, n_seq, L):
    low = lax.broadcasted_iota(jnp.int32, (1, LANES), 1) < HEAD_DIM
    masks = (low, jnp.logical_not(low))

    def seq_body(s, carry):
        rows = pl.ds(pl.multiple_of(s * L, L), L)
        for p in range(ATTN_W // LANES):
            cols = lambda part: slice(part * ATTN_W + p * LANES, part * ATTN_W + (p + 1) * LANES)
            qb = qkv_ref[rows, cols(0)]
            kb = qkv_ref[rows, cols(1)]
            vb = qkv_ref[rows, cols(2)]
            outs = []
            for m in masks:
                s_ = _dot_t(qb, jnp.where(m, kb, jnp.zeros_like(kb)))
                p_ = jnp.exp(s_ - jnp.max(s_, axis=-1, keepdims=True))
                outs.append(_dot(p_.astype(BF16), vb) * (1.0 / jnp.sum(p_, axis=-1, keepdims=True)))
            o_ref[rows, p * LANES:(p + 1) * LANES] = jnp.where(low, outs[0], outs[1])
        return carry

    lax.fori_loop(0, n_seq, seq_body, 0)


def _ctx_attn(qkv, L):
    n = qkv.shape[0]
    return pl.pallas_call(
        functools.partial(_ctx_attn_kernel, n_seq=ROW_BLOCK // L, L=L),
        grid=(n // ROW_BLOCK,),
        in_specs=[pl.BlockSpec((ROW_BLOCK, 3 * ATTN_W), lambda i: (i, 0))],
        out_specs=pl.BlockSpec((ROW_BLOCK, ATTN_W), lambda i: (i, 0)),
        out_shape=jax.ShapeDtypeStruct((n, ATTN_W), F32),
        compiler_params=_cparams(("arbitrary",)),
        name="ctx_attn",
    )(qkv)


N_DR = 2 * WIN_ROWS - 1
N_DC = 2 * WIN_COLS - 1


def _rpb_expand_table():
    d = np.clip(np.arange(LANES) - GRID_W, -(WIN_COLS - 1), WIN_COLS - 1) + (WIN_COLS - 1)
    t = np.zeros((32, LANES), np.float32)
    t[d, np.arange(LANES)] = 1.0
    return t


def _bias_tile_index(n_rows):
    n_blk = n_rows // Q_ROWS
    idx = np.full((n_blk, Q_ROWS, KEY_ROWS), N_DR, np.int32)
    for i in range(n_blk):
        ks = min(max(Q_ROWS * i - WIN_ROWS // 2, 0), n_rows - KEY_ROWS)
        for a in range(Q_ROWS):
            r = Q_ROWS * i + a
            rs = min(max(r - WIN_ROWS // 2, 0), n_rows - WIN_ROWS)
            for j in range(KEY_ROWS):
                if rs <= ks + j < rs + WIN_ROWS:
                    idx[i, a, j] = ks + j - r + (WIN_ROWS - 1)
            assert (idx[i, a] != N_DR).sum() == WIN_ROWS, "key window must cover the whole band"
    return idx


def _build_bias(rpb_ref, ext_ref, tw_ref, bias_ref, n_blk):
    shape = (GRID_W, LANES)
    lane = lax.broadcasted_iota(jnp.int32, shape, 1)
    qc = lax.broadcasted_iota(jnp.int32, shape, 0)
    kc = lane & (GRID_W - 1)
    low = lane < GRID_W
    col_start = jnp.clip(qc - WIN_COLS // 2, 0, GRID_W - WIN_COLS)
    col_in = (kc >= col_start) & (kc < col_start + WIN_COLS)
    idx = _bias_tile_index(Q_ROWS * n_blk)
    for e in range(2):
        g = jnp.dot(rpb_ref[e], ext_ref[...], precision=HIGHEST, preferred_element_type=F32)
        for dr in range(N_DR):
            t = pltpu.roll(jnp.broadcast_to(g[dr:dr + 1, :], shape), GRID_W, 1, stride=1, stride_axis=0)
            t = jnp.where(low, t, pltpu.roll(t, GRID_W, 1))
            tw_ref[e, dr] = jnp.where(col_in, t, NEG)
        tw_ref[e, N_DR] = jnp.full(shape, NEG, F32)
        for i in range(n_blk):
            for a in range(Q_ROWS):
                for jt in range(KEY_ROWS // 2):
                    tile = jnp.where(low, tw_ref[e, int(idx[i, a, 2 * jt])], tw_ref[e, int(idx[i, a, 2 * jt + 1])])
                    bias_ref[e, i, a * GRID_W:(a + 1) * GRID_W, jt * LANES:(jt + 1) * LANES] = tile


def _lat_attn_kernel(q_ref, k_ref, v_ref, kc_ref, vc_ref, rpb_ref, ext_ref, o_ref, km_ref, vm_ref, tw_ref,
                     bias_ref, *, n_blk):
    @pl.when(pl.program_id(1) == 0)
    def _():
        _build_bias(rpb_ref, ext_ref, tw_ref, bias_ref, n_blk)

    low = lax.broadcasted_iota(jnp.int32, (1, LANES), 1) < HEAD_DIM
    masks = (low, jnp.logical_not(low))
    ks, vs = _split_heads(k_ref[...], v_ref[...], masks)
    for e in range(2):
        km_ref[e] = ks[e]
        vm_ref[e] = vs[e]
    kce, vce = _split_heads(kc_ref[0], vc_ref[0], masks)
    nq = Q_ROWS * GRID_W
    nk = KEY_ROWS * GRID_W
    for i in range(n_blk):
        qrows = slice(i * nq, (i + 1) * nq)
        k0 = min(max(Q_ROWS * i - WIN_ROWS // 2, 0), Q_ROWS * n_blk - KEY_ROWS) * GRID_W
        krows = slice(k0, k0 + nk)
        qb = q_ref[qrows, :]
        outs = []
        for e in range(2):
            s_loc = _dot_t(qb, km_ref[e, krows, :]) + bias_ref[e, i]
            s_ctx = _dot_t(qb, kce[e])
            outs.append(_softmax_pv([s_loc, s_ctx], [vm_ref[e, krows, :], vce[e]]))
        o_ref[qrows, :] = _normalise_heads(outs[0], outs[1], low)


def _lat_attn(qkv, kc, vc, rpb, batch, L):
    n_blk = L // (Q_ROWS * GRID_W)
    n_pairs = ATTN_W // LANES
    col = lambda part: pl.BlockSpec((L, LANES), lambda p, b: (b, part * n_pairs + p))
    cache = pl.BlockSpec((1, kc.shape[1], LANES), lambda p, b: (b, 0, p))
    rpb_pad = jnp.pad(rpb, ((0, 0), (0, 16 - N_DR), (0, 32 - N_DC)))
    return pl.pallas_call(
        functools.partial(_lat_attn_kernel, n_blk=n_blk),
        grid=(n_pairs, batch),
        in_specs=[col(0), col(1), col(2), cache, cache,
                  pl.BlockSpec((2, 16, 32), lambda p, b: (p, 0, 0)),
                  pl.BlockSpec((32, LANES), lambda p, b: (0, 0))],
        out_specs=pl.BlockSpec((L, LANES), lambda p, b: (b, p)),
        out_shape=jax.ShapeDtypeStruct((batch * L, ATTN_W), F32),
        scratch_shapes=[pltpu.VMEM((2, L, LANES), BF16), pltpu.VMEM((2, L, LANES), BF16),
                        pltpu.VMEM((2, 16, GRID_W, LANES), F32),
                        pltpu.VMEM((2, n_blk, Q_ROWS * GRID_W, KEY_ROWS * GRID_W), F32)],
        compiler_params=_cparams(("arbitrary", "arbitrary")),
        name="lat_attn",
    )(qkv, qkv, qkv, kc, vc, rpb_pad, jnp.asarray(_rpb_expand_table()))


SUBLANES = 8


def _row_neighbours(x, L):
    n, c = x.shape
    t, tl = n // SUBLANES, L // SUBLANES
    x3 = x.reshape(t, SUBLANES, c)
    sub = lax.broadcasted_iota(jnp.int32, (1, SUBLANES, 1), 1)
    zero = jnp.zeros((1, SUBLANES, c), x.dtype)
    down = pltpu.roll(x3, 1, 1)
    up = pltpu.roll(x3, SUBLANES - 1, 1)
    down_prev = jnp.concatenate([p for s in range(0, t, tl) for p in (zero, down[s:s + tl - 1])], axis=0)
    up_next = jnp.concatenate([p for s in range(0, t, tl) for p in (up[s + 1:s + tl], zero)], axis=0)
    above = jnp.where(sub == 0, down_prev, down).reshape(n, c)
    below = jnp.where(sub == SUBLANES - 1, up_next, up).reshape(n, c)
    return above, below


def _dwconv3(x, w_ref, b_ref, L):
    above, below = _row_neighbours(x, L)
    return above * w_ref[0:1, :] + x * w_ref[1:2, :] + below * w_ref[2:3, :] + b_ref[...]


def _hyena_kernel(h1_ref, h2_ref, hv_ref, w1_ref, w2_ref, wv_ref, b1_ref, b2_ref, bv_ref,
                  kf_ref, fb_ref, a_ref, ai_ref, o_ref, *, L):
    row0 = lax.broadcasted_iota(jnp.int32, (L, 1), 0) == 0

    def long_conv(u, order):
        ub = u.astype(BF16)
        ur = _dot(a_ref[0:L, :], ub)
        ui = _dot(a_ref[L:2 * L, :], ub)
        kr = kf_ref[order, 0:L, :]
        ki = kf_ref[order, L:2 * L, :]
        yr = jnp.where(row0, ur * kr, ur * kr - ui * ki)
        yi = jnp.where(row0, ui * ki, ur * ki + ui * kr)
        y = _dot(ai_ref[:, 0:L], yr.astype(BF16)) + _dot(ai_ref[:, L:2 * L], yi.astype(BF16))
        return y + u * fb_ref[order:order + 1, :]

    def one_sequence(rows):
        v = _dwconv3(hv_ref[rows, :], wv_ref, bv_ref, L)
        z = _dwconv3(h1_ref[rows, :], w1_ref, b1_ref, L) * long_conv(v, 0)
        o_ref[rows, :] = _dwconv3(h2_ref[rows, :], w2_ref, b2_ref, L) * long_conv(z, 1)

    n_seq = h1_ref.shape[0] // L
    if n_seq == 1:
        one_sequence(slice(0, L))
    else:
        def seq_body(s, carry):
            one_sequence(pl.ds(pl.multiple_of(s * L, L), L))
            return carry

        lax.fori_loop(0, n_seq, seq_body, 0)


def _hyena(hy, conv_w, conv_b, kf, filt_bias, a_fwd, a_inv, L):
    n = hy.shape[0]
    part = lambda k, rows: pl.BlockSpec((rows, HY_CH), lambda i: (i if rows == ROW_BLOCK else 0, k))
    once = lambda shape: pl.BlockSpec(shape, lambda i: (0,) * len(shape), pipeline_mode=pl.Buffered(1))
    return pl.pallas_call(
        functools.partial(_hyena_kernel, L=L),
        grid=(n // ROW_BLOCK,),
        in_specs=[part(0, ROW_BLOCK), part(1, ROW_BLOCK), part(2, ROW_BLOCK),
                  part(0, 3), part(1, 3), part(2, 3),
                  part(0, 1), part(1, 1), part(2, 1),
                  once((2, 2 * L, HY_CH)), once((2, HY_CH)), once((2 * L, L)), once((L, 2 * L))],
        out_specs=pl.BlockSpec((ROW_BLOCK, HY_CH), lambda i: (i, 0)),
        out_shape=jax.ShapeDtypeStruct((n, HY_CH), F32),
        compiler_params=_cparams(("arbitrary",)),
        name=f"hyena_{L}",
    )(hy, hy, hy, conv_w, conv_w, conv_w, conv_b, conv_b, conv_b, kf, filt_bias, a_fwd, a_inv)


def _tail_kernel(*refs, L, mod_base, blocks_per_mod, halo):
    rows_refs, rest = refs[:9 if halo else 3], refs[9 if halo else 3:]
    (mods_ref, gg_ref, wo_ref, n2_ref, fg_ref, wu_ref, cw_ref, cb_ref, wd_ref, o_ref,
     h2_s, acc_s, ma_s, mb_s) = rest
    rows = o_ref.shape[0]
    i = pl.program_id(0)
    r = mod_base + i // blocks_per_mod
    mod = lambda k: mods_ref[k, pl.ds(r, 1), :]

    if halo:
        x_ref, a_ref, hy_ref, xp_ref, ap_ref, hp_ref, xn_ref, an_ref, hn_ref = rows_refs
        cat = lambda p, m, n: jnp.concatenate([p[...], m[...], n[...]], axis=0)
        x, a, hy = cat(xp_ref, x_ref, xn_ref), cat(ap_ref, a_ref, an_ref), cat(hp_ref, hy_ref, hn_ref)
    else:
        x, a, hy = (ref[...] for ref in rows_refs)
    gg = gg_ref[...]
    merged = jnp.concatenate([_rms(a) * gg[:, :ATTN_W], _rms(hy) * gg[:, ATTN_W:]], axis=-1).astype(BF16)
    x1 = x + mod(2) * _dot(merged, wo_ref[...])
    h2_s[...] = (_rms(x1) * n2_ref[...] * (1.0 + mod(4)) + mod(3)).astype(BF16)
    o_ref[...] = x1[halo:halo + rows]

    ext = rows + 2 * halo
    base = TAIL_PAD + halo
    n_tiles, seq_tiles = rows // SUBLANES, L // SUBLANES
    sub = lax.broadcasted_iota(jnp.int32, (1, SUBLANES, 1), 1)
    zero_tile = jnp.zeros((1, SUBLANES, FF_TILE), F32)
    starts_seq = (i * rows) & (L - 1) == 0
    ends_seq = ((i + 1) * rows) & (L - 1) == 0

    def up_proj(j, dst):
        for e in range(2):
            cols = pl.ds(pl.multiple_of(e * D_FF + j * FF_TILE, FF_TILE), FF_TILE)
            dst[e, TAIL_PAD:TAIL_PAD + ext, :] = _dot(h2_s[...], wu_ref[:, cols])

    def conv(src, e, j):
        t = src[e, base - SUBLANES:base + rows + SUBLANES, :].reshape(n_tiles + 2, SUBLANES, FF_TILE)
        down = pltpu.roll(t, 1, 1)
        up = pltpu.roll(t, SUBLANES - 1, 1)
        if L >= rows:
            above = [jnp.where(starts_seq, zero_tile, down[0:1]), down[1:n_tiles]]
            below = [up[2:n_tiles + 1], jnp.where(ends_seq, zero_tile, up[n_tiles + 1:n_tiles + 2])]
        else:
            above, below = [], []
            for s in range(0, n_tiles, seq_tiles):
                above += [down[0:1] if s == 0 else zero_tile, down[s + 1:s + seq_tiles]]
                last_seq = s + seq_tiles == n_tiles
                below += [up[s + 2:s + seq_tiles + 1], up[n_tiles + 1:n_tiles + 2] if last_seq else zero_tile]
        above = jnp.where(sub == 0, jnp.concatenate(above, axis=0), down[1:n_tiles + 1])
        below = jnp.where(sub == SUBLANES - 1, jnp.concatenate(below, axis=0), up[1:n_tiles + 1])
        w = cw_ref[e, j]
        return above * w[0:1] + t[1:n_tiles + 1] * w[1:2] + below * w[2:3] + cb_ref[e, j]

    def down_proj(j, src):
        gate = conv(src, 0, j)
        act = gate / (1.0 + jnp.exp(-gate)) * conv(src, 1, j)
        return _dot(act.reshape(rows, FF_TILE).astype(BF16), wd_ref[j])

    for buf in (ma_s, mb_s):
        for e in range(2):
            buf[e, 0:TAIL_PAD, :] = jnp.zeros((TAIL_PAD, FF_TILE), F32)
            buf[e, TAIL_PAD + ext:, :] = jnp.zeros((TAIL_PAD, FF_TILE), F32)
    up_proj(0, ma_s)
    acc_s[...] = jnp.zeros_like(acc_s)

    def two_tiles(k, carry):
        j = 2 * k
        up_proj(j + 1, mb_s)
        acc_s[...] += down_proj(j, ma_s)
        up_proj(j + 2, ma_s)
        acc_s[...] += down_proj(j + 1, mb_s)
        return carry

    lax.fori_loop(0, (N_FF_TILES - 1) // 2, two_tiles, 0)
    x2 = o_ref[...] + mod(5) * (acc_s[...] + down_proj(N_FF_TILES - 1, ma_s))
    o_ref[...] = _rms(x2) * fg_ref[...]


def _tail(x, a, hyo, mods, grp_g, w_out, n2_g, final_g, w_up, conv_w, conv_b, w_down, *, L, mod_base,
          blocks_per_mod):
    n = x.shape[0]
    halo = TAIL_HALO if L > TAIL_ROWS else 0
    row = lambda w: pl.BlockSpec((TAIL_ROWS, w), lambda i: (i, 0))
    per_halo = TAIL_ROWS // TAIL_HALO
    prev = lambda w: pl.BlockSpec((TAIL_HALO, w), lambda i: (jnp.maximum(i * per_halo - 1, 0), 0))
    nxt = lambda w: pl.BlockSpec((TAIL_HALO, w), lambda i: (jnp.minimum((i + 1) * per_halo, n // TAIL_HALO - 1), 0))
    once = lambda shape: pl.BlockSpec(shape, lambda i: (0,) * len(shape), pipeline_mode=pl.Buffered(1))
    widths = (D_MODEL, ATTN_W, HY_CH)
    row_specs = [row(w) for w in widths]
    row_args = [x, a, hyo]
    if halo:
        row_specs += [prev(w) for w in widths] + [nxt(w) for w in widths]
        row_args += [x, a, hyo, x, a, hyo]
    ext = TAIL_ROWS + 2 * halo
    return pl.pallas_call(
        functools.partial(_tail_kernel, L=L, mod_base=mod_base, blocks_per_mod=blocks_per_mod, halo=halo),
        grid=(n // TAIL_ROWS,),
        in_specs=row_specs + [
            once((6, 8, D_MODEL)), once((1, D_MODEL)), once((D_MODEL, D_MODEL)), once((1, D_MODEL)),
            once((1, D_MODEL)),
            once((D_MODEL, 2 * D_FF)),
            once((2, N_FF_TILES, 3, FF_TILE)), once((2, N_FF_TILES, 1, FF_TILE)),
            once((N_FF_TILES, FF_TILE, D_MODEL))],
        out_specs=row(D_MODEL),
        out_shape=jax.ShapeDtypeStruct((n, D_MODEL), F32),
        scratch_shapes=[pltpu.VMEM((ext, D_MODEL), BF16), pltpu.VMEM((TAIL_ROWS, D_MODEL), F32),
                        pltpu.VMEM((2, ext + 2 * TAIL_PAD, FF_TILE), F32),
                        pltpu.VMEM((2, ext + 2 * TAIL_PAD, FF_TILE), F32)],
        compiler_params=_cparams(("arbitrary",)),
        name=f"tail_{L}",
    )(*row_args, mods, grp_g, w_out, n2_g, final_g, w_up, conv_w, conv_b, w_down)


def kernel(x_prompt, x_sample, cache_ctx_k, cache_ctx_v, c, c_ctx, w_ada, b_ada, norm1_g, w_in, rpb,
           hy_conv_w, hy_conv_b, filt_w1, filt_b1, filt_w2, filt_b2, filt_w3, filt_b3, filt_freq,
           filt_bias, grp_norm_g, w_out, norm2_g, w_up, ffn_conv_w, ffn_conv_b, w_down, final_g):
    assert w_ada.shape[0] == 1, "single-layer trunk"
    bc, lc, _ = x_prompt.shape
    bl, ll, _ = x_sample.shape
    past = cache_ctx_k.shape[3]

    cvec = jnp.concatenate([c_ctx[None], c, jnp.zeros((8 - 1 - bl, D_MODEL), F32)], axis=0)
    mods = _mods(cvec, w_ada[0], b_ada[0])

    w_in_b = w_in[0].astype(BF16)
    w_out_b = w_out[0].astype(BF16)
    w_down_t = w_down[0].astype(BF16).reshape(N_FF_TILES, FF_TILE, D_MODEL)
    conv_w_t = ffn_conv_w[0].reshape(3, 2, N_FF_TILES, FF_TILE).transpose(1, 2, 0, 3)
    conv_b_t = ffn_conv_b[0].reshape(2, N_FF_TILES, 1, FF_TILE)
    g1 = norm1_g[0][None]
    shared_tail = (grp_norm_g[0][None], w_out_b, norm2_g[0][None], final_g[None], w_up[0].astype(BF16),
                   conv_w_t, conv_b_t, w_down_t)
    filt = (filt_w1[0], filt_b1[0], filt_w2[0], filt_b2[0], filt_w3[0], filt_b3[0], filt_freq[0])

    def tables(L):
        fwd, inv = _dft_tables(L)
        return jnp.asarray(fwd).astype(BF16), jnp.asarray(inv).astype(BF16)

    xc = x_prompt.reshape(bc * lc, D_MODEL)
    a_fwd, a_inv = tables(lc)
    kf = _filter_spectrum(lc, a_fwd, *filt)
    qkv, hy, state_k, state_v = _proj(xc, mods, g1, w_in_b, mod_base=0, blocks_per_mod=bc * lc // PROJ_ROWS,
                                      L=lc, emit_state=True)
    att = _ctx_attn(qkv, lc)
    hyo = _hyena(hy, hy_conv_w[0], hy_conv_b[0][None], kf, filt_bias[0], a_fwd, a_inv, lc)
    y_prompt = _tail(xc, att, hyo, mods, *shared_tail, L=lc, mod_base=0,
                     blocks_per_mod=bc * lc // TAIL_ROWS).reshape(bc, lc, D_MODEL)

    xs = x_sample.reshape(bl * ll, D_MODEL)
    a_fwd, a_inv = tables(ll)
    kf = _filter_spectrum(ll, a_fwd, *filt)
    qkv, hy = _proj(xs, mods, g1, w_in_b, mod_base=1, blocks_per_mod=ll // PROJ_ROWS, L=ll, emit_state=False)
    lanes_major = lambda t: t[:, 0].transpose(0, 2, 1, 3).reshape(bl, past, ATTN_W).astype(BF16)
    att = _lat_attn(qkv, lanes_major(cache_ctx_k), lanes_major(cache_ctx_v), rpb[0], bl, ll)
    hyo = _hyena(hy, hy_conv_w[0], hy_conv_b[0][None], kf, filt_bias[0], a_fwd, a_inv, ll)
    y_sample = _tail(xs, att, hyo, mods, *shared_tail, L=ll, mod_base=1,
                     blocks_per_mod=ll // TAIL_ROWS).reshape(bl, ll, D_MODEL)

    return (y_prompt, y_sample, state_k, state_v)
```

```python
import functools
import math

import numpy as np
import jax
import jax.numpy as jnp
from jax import lax
from jax.experimental import pallas as pl
from jax.experimental.pallas import tpu as pltpu

F32 = jnp.float32
BF16 = jnp.bfloat16
HIGHEST = lax.Precision.HIGHEST

D_MODEL = 1024
N_HEADS = 8
HEAD_DIM = 64
ATTN_W = N_HEADS * HEAD_DIM
HY_CH = D_MODEL - ATTN_W
IN_COLS = 3 * ATTN_W + 3 * HY_CH
D_FF = 2816
GRID_W = 64
WIN_ROWS = 8
WIN_COLS = 16
FILT_FREQS = 8
FILT_HID = 64
FILT_EMB_PAD = 32
DECAY_TARGET = 1e-2
MAX_DECAY = math.log(DECAY_TARGET) / 0.3
MIN_DECAY = math.log(DECAY_TARGET) / 1.5
EPS = 1e-6
NEG = -1e30

LANES = 128
ROW_BLOCK = 1024
PROJ_ROWS = 512
TAIL_ROWS = 512
TAIL_HALO = 16
TAIL_PAD = 8
FF_TILE = 256
N_FF_TILES = D_FF // FF_TILE
assert N_FF_TILES % 2 == 1 and N_FF_TILES >= 3, "the ConvFFN pipeline peels an odd tile count"
Q_ROWS = 4
KEY_ROWS = 12
VMEM_LIMIT = 56 * 1024 * 1024


def _cparams(sem):
    return pltpu.CompilerParams(dimension_semantics=sem, vmem_limit_bytes=VMEM_LIMIT)


def _rms(x):
    return x * lax.rsqrt(jnp.mean(x * x, axis=-1, keepdims=True) + EPS)


def _dot(a, b):
    return jnp.dot(a, b, preferred_element_type=F32)


def _dot3(a, b):
    ah = a.astype(BF16)
    bh = b.astype(BF16)
    al = (a - ah.astype(F32)).astype(BF16)
    bl = (b - bh.astype(F32)).astype(BF16)
    return _dot(ah, bh) + _dot(al, bh) + _dot(ah, bl)


def _dot_t(a, b):
    return lax.dot_general(a, b, (((1,), (1,)), ((), ())), preferred_element_type=F32)


def _mods_kernel(c_ref, w_ref, b_ref, o_ref):
    cv = c_ref[...]
    s = cv / (1.0 + jnp.exp(-cv))
    o_ref[0] = _dot3(s, w_ref[...]) + b_ref[0]


def _mods(cvec, w_ada, b_ada):
    return pl.pallas_call(
        _mods_kernel,
        grid=(6,),
        in_specs=[
            pl.BlockSpec((8, D_MODEL), lambda j: (0, 0)),
            pl.BlockSpec((D_MODEL, D_MODEL), lambda j: (0, j)),
            pl.BlockSpec((1, 1, D_MODEL), lambda j: (j, 0, 0)),
        ],
        out_specs=pl.BlockSpec((1, 8, D_MODEL), lambda j: (j, 0, 0)),
        out_shape=jax.ShapeDtypeStruct((6, 8, D_MODEL), F32),
        compiler_params=_cparams(("arbitrary",)),
        name="mods",
    )(cvec, w_ada, b_ada.reshape(6, 1, D_MODEL))


def _dft_tables(L):
    f = np.arange(L, dtype=np.int64)[:, None]
    t = np.arange(L, dtype=np.int64)[None, :]
    ang = np.pi * ((f * t) % (2 * L)).astype(np.float64) / L
    top = np.cos(ang)
    bot = -np.sin(ang)
    bot[0, :] = np.where(np.arange(L) % 2 == 0, 1.0, -1.0)
    fwd = np.concatenate([top, bot], axis=0)
    w = np.full((2 * L,), 1.0 / L)
    w[0] = w[L] = 0.5 / L
    inv = (fwd * w[:, None]).T
    return fwd.astype(np.float32), inv.astype(np.float32)


def _filter_consts(L):
    t = np.arange(L, dtype=np.float64) / L
    fr = np.arange(1, FILT_FREQS + 1, dtype=np.float64)
    ang = 2.0 * math.pi * fr[:, None] * t[None, :]
    z = np.zeros((FILT_EMB_PAD, L), np.float64)
    z[0] = t
    z[1:1 + FILT_FREQS] = np.cos(ang)
    z[1 + FILT_FREQS:1 + 2 * FILT_FREQS] = np.sin(ang)
    deltas = np.abs(np.linspace(MIN_DECAY, MAX_DECAY, HY_CH))
    decay = np.exp(-t[:, None] * deltas[None, :])
    return z.astype(np.float32), decay.astype(np.float32)


def _filter_kernel(z_ref, w1_ref, b1_ref, w2_ref, b2_ref, w3_ref, b3_ref, fr_ref, dec_ref,
                   a_ref, o_ref, h_s, *, L):
    @pl.when(pl.program_id(0) == 0)
    def _():
        fr = fr_ref[...]
        h = jnp.sin(fr * (jnp.dot(w1_ref[...], z_ref[...], precision=HIGHEST,
                                  preferred_element_type=F32) + b1_ref[...]))
        h_s[...] = jnp.sin(fr * (jnp.dot(w2_ref[...], h, precision=HIGHEST,
                                         preferred_element_type=F32) + b2_ref[...]))

    tdot = lambda a, b: lax.dot_general(a, b, (((0,), (0,)), ((), ())), preferred_element_type=F32)
    hid, w3 = h_s[...], w3_ref[...]
    hh, wh = hid.astype(BF16), w3.astype(BF16)
    hl, wl = (hid - hh.astype(F32)).astype(BF16), (w3 - wh.astype(F32)).astype(BF16)
    h = tdot(hh, wh) + tdot(hl, wh) + tdot(hh, wl) + b3_ref[...]
    dec = dec_ref[...]
    row0 = lax.broadcasted_iota(jnp.int32, (L, 1), 0) == 0
    fwd = h[:, :HY_CH] * dec
    bwd = jnp.where(row0, 0.0, h[:, HY_CH:] * dec)
    gp = (fwd + bwd).astype(BF16)
    gm = (fwd - bwd).astype(BF16)
    top = _dot(a_ref[0:L, :], gp)
    bot = _dot(a_ref[L:2 * L, :], gm)
    nyq = _dot(a_ref[L:L + 16, :], gp)[0:1]
    o_ref[0, 0:L, :] = top
    o_ref[0, L:2 * L, :] = jnp.where(row0, nyq, bot)


def _filter_spectrum(L, a_fwd, w1, b1, w2, b2, w3, b3, freq):
    z, decay = _filter_consts(L)
    full = lambda shape: pl.BlockSpec(shape, lambda o: (0,) * len(shape))
    col = lambda v: v[:, None]
    return pl.pallas_call(
        functools.partial(_filter_kernel, L=L),
        grid=(2,),
        in_specs=[
            full((FILT_EMB_PAD, L)),
            full((FILT_HID, FILT_EMB_PAD)),
            full((FILT_HID, 1)),
            full((FILT_HID, FILT_HID)),
            full((FILT_HID, 1)),
            pl.BlockSpec((FILT_HID, 2 * HY_CH), lambda o: (0, o)),
            pl.BlockSpec((1, 2 * HY_CH), lambda o: (0, o)),
            full((FILT_HID, 1)),
            full((L, HY_CH)),
            full((2 * L, L)),
        ],
        out_specs=pl.BlockSpec((1, 2 * L, HY_CH), lambda o: (o, 0, 0)),
        out_shape=jax.ShapeDtypeStruct((2, 2 * L, HY_CH), F32),
        scratch_shapes=[pltpu.VMEM((FILT_HID, L), F32)],
        compiler_params=_cparams(("arbitrary",)),
        name=f"filter_{L}",
    )(jnp.asarray(z), jnp.pad(w1, ((0, FILT_EMB_PAD - w1.shape[0]), (0, 0))).T, col(b1), w2.T, col(b2),
      w3, b3[None], col(freq), jnp.asarray(decay), a_fwd)


def _store_heads_major(x, dst_ref, L):
    for s in range(x.shape[0] // L):
        for h in range(N_HEADS):
            dst_ref[s, 0, h] = x[s * L:(s + 1) * L, h * HEAD_DIM:(h + 1) * HEAD_DIM]


def _proj_kernel(x_ref, mods_ref, g_ref, w_ref, qkv_ref, hy_ref, *state, mod_base, blocks_per_mod, L):
    r = mod_base + pl.program_id(0) // blocks_per_mod
    sh1 = mods_ref[0, pl.ds(r, 1), :]
    sc1 = mods_ref[1, pl.ds(r, 1), :]
    h = (_rms(x_ref[...]) * g_ref[...] * (1.0 + sc1) + sh1).astype(BF16)
    q = _dot(h, w_ref[:, 0:ATTN_W])
    qkv_ref[:, 0:ATTN_W] = (q * HEAD_DIM ** -0.5).astype(BF16)
    k = _dot(h, w_ref[:, ATTN_W:2 * ATTN_W])
    qkv_ref[:, ATTN_W:2 * ATTN_W] = k.astype(BF16)
    v = _dot(h, w_ref[:, 2 * ATTN_W:3 * ATTN_W])
    qkv_ref[:, 2 * ATTN_W:3 * ATTN_W] = v.astype(BF16)
    if state:
        ks_ref, vs_ref = state
        _store_heads_major(k, ks_ref, L)
        _store_heads_major(v, vs_ref, L)
    hy_ref[...] = _dot(h, w_ref[:, 3 * ATTN_W:])


def _proj(x, mods, g, w_in, *, mod_base, blocks_per_mod, L, emit_state):
    n = x.shape[0]
    row = lambda w: pl.BlockSpec((PROJ_ROWS, w), lambda i: (i, 0))
    out_specs = [row(3 * ATTN_W), row(3 * HY_CH)]
    out_shape = [jax.ShapeDtypeStruct((n, 3 * ATTN_W), BF16), jax.ShapeDtypeStruct((n, 3 * HY_CH), F32)]
    if emit_state:
        seqs = PROJ_ROWS // L
        state = pl.BlockSpec((seqs, 1, N_HEADS, L, HEAD_DIM), lambda i: (i, 0, 0, 0, 0))
        out_specs += [state, state]
        out_shape += [jax.ShapeDtypeStruct((n // L, 1, N_HEADS, L, HEAD_DIM), F32)] * 2
    return pl.pallas_call(
        functools.partial(_proj_kernel, mod_base=mod_base, blocks_per_mod=blocks_per_mod, L=L),
        grid=(n // PROJ_ROWS,),
        in_specs=[
            row(D_MODEL),
            pl.BlockSpec((6, 8, D_MODEL), lambda i: (0, 0, 0)),
            pl.BlockSpec((1, D_MODEL), lambda i: (0, 0)),
            pl.BlockSpec((D_MODEL, IN_COLS), lambda i: (0, 0)),
        ],
        out_specs=out_specs,
        out_shape=out_shape,
        compiler_params=_cparams(("arbitrary",)),
        name="proj_ctx" if emit_state else "proj_lat",
    )(x, mods, g, w_in)


def _softmax_pv(scores, values):
    mx = functools.reduce(jnp.maximum, [jnp.max(s, axis=-1, keepdims=True) for s in scores])
    return functools.reduce(jnp.add, [_dot(jnp.exp(s - mx).astype(BF16), v) for s, v in zip(scores, values)])


def _normalise_heads(o0, o1, low):
    den = pltpu.roll(jnp.where(low, o1, o0), HEAD_DIM, 1)
    return jnp.where(low, o0, o1) * (1.0 / den)


def _split_heads(k, v, masks):
    ks = [jnp.where(m, k, jnp.zeros_like(k)) for m in masks]
    vs = [jnp.where(m, v, jnp.ones_like(v)) for m in masks]
    return ks, vs


def _ctx_attn_kernel(qkv_ref, o_ref, *, n_seq, L):
    low = lax.broadcasted_iota(jnp.int32, (1, LANES), 1) < HEAD_DIM
    masks = (low, jnp.logical_not(low))

    def seq_body(s, carry):
        rows = pl.ds(pl.multiple_of(s * L, L), L)
        for p in range(ATTN_W // LANES):
            cols = lambda part: slice(part * ATTN_W + p * LANES, part * ATTN_W + (p + 1) * LANES)
            qb = qkv_ref[rows, cols(0)]
            kb = qkv_ref[rows, cols(1)]
            vb = qkv_ref[rows, cols(2)]
            outs = []
            for m in masks:
                s_ = _dot_t(qb, jnp.where(m, kb, jnp.zeros_like(kb)))
                p_ = jnp.exp(s_ - jnp.max(s_, axis=-1, keepdims=True))
                outs.append(_dot(p_.astype(BF16), vb) * (1.0 / jnp.sum(p_, axis=-1, keepdims=True)))
            o_ref[rows, p * LANES:(p + 1) * LANES] = jnp.where(low, outs[0], outs[1])
        return carry

    lax.fori_loop(0, n_seq, seq_body, 0)


def _ctx_attn(qkv, L):
    n = qkv.shape[0]
    return pl.pallas_call(
        functools.partial(_ctx_attn_kernel, n_seq=ROW_BLOCK // L, L=L),
        grid=(n // ROW_BLOCK,),
        in_specs=[pl.BlockSpec((ROW_BLOCK, 3 * ATTN_W), lambda i: (i, 0))],
        out_specs=pl.BlockSpec((ROW_BLOCK, ATTN_W), lambda i: (i, 0)),
        out_shape=jax.ShapeDtypeStruct((n, ATTN_W), F32),
        compiler_params=_cparams(("arbitrary",)),
        name="ctx_attn",
    )(qkv)


N_DR = 2 * WIN_ROWS - 1
N_DC = 2 * WIN_COLS - 1


def _rpb_expand_table():
    d = np.clip(np.arange(LANES) - GRID_W, -(WIN_COLS - 1), WIN_COLS - 1) + (WIN_COLS - 1)
    t = np.zeros((32, LANES), np.float32)
    t[d, np.arange(LANES)] = 1.0
    return t


def _bias_tile_index(n_rows):
    n_blk = n_rows // Q_ROWS
    idx = np.full((n_blk, Q_ROWS, KEY_ROWS), N_DR, np.int32)
    for i in range(n_blk):
        ks = min(max(Q_ROWS * i - WIN_ROWS // 2, 0), n_rows - KEY_ROWS)
        for a in range(Q_ROWS):
            r = Q_ROWS * i + a
            rs = min(max(r - WIN_ROWS // 2, 0), n_rows - WIN_ROWS)
            for j in range(KEY_ROWS):
                if rs <= ks + j < rs + WIN_ROWS:
                    idx[i, a, j] = ks + j - r + (WIN_ROWS - 1)
            assert (idx[i, a] != N_DR).sum() == WIN_ROWS, "key window must cover the whole band"
    return idx


def _build_bias(rpb_ref, ext_ref, tw_ref, bias_ref, n_blk):
    shape = (GRID_W, LANES)
    lane = lax.broadcasted_iota(jnp.int32, shape, 1)
    qc = lax.broadcasted_iota(jnp.int32, shape, 0)
    kc = lane & (GRID_W - 1)
    low = lane < GRID_W
    col_start = jnp.clip(qc - WIN_COLS // 2, 0, GRID_W - WIN_COLS)
    col_in = (kc >= col_start) & (kc < col_start + WIN_COLS)
    idx = _bias_tile_index(Q_ROWS * n_blk)
    for e in range(2):
        g = jnp.dot(rpb_ref[e], ext_ref[...], precision=HIGHEST, preferred_element_type=F32)
        for dr in range(N_DR):
            t = pltpu.roll(jnp.broadcast_to(g[dr:dr + 1, :], shape), GRID_W, 1, stride=1, stride_axis=0)
            t = jnp.where(low, t, pltpu.roll(t, GRID_W, 1))
            tw_ref[e, dr] = jnp.where(col_in, t, NEG)
        tw_ref[e, N_DR] = jnp.full(shape, NEG, F32)
        for i in range(n_blk):
            for a in range(Q_ROWS):
                for jt in range(KEY_ROWS // 2):
                    tile = jnp.where(low, tw_ref[e, int(idx[i, a, 2 * jt])], tw_ref[e, int(idx[i, a, 2 * jt + 1])])
                    bias_ref[e, i, a * GRID_W:(a + 1) * GRID_W, jt * LANES:(jt + 1) * LANES] = tile


def _lat_attn_kernel(q_ref, k_ref, v_ref, kc_ref, vc_ref, rpb_ref, ext_ref, o_ref, km_ref, vm_ref, tw_ref,
                     bias_ref, *, n_blk):
    @pl.when(pl.program_id(1) == 0)
    def _():
        _build_bias(rpb_ref, ext_ref, tw_ref, bias_ref, n_blk)

    low = lax.broadcasted_iota(jnp.int32, (1, LANES), 1) < HEAD_DIM
    masks = (low, jnp.logical_not(low))
    ks, vs = _split_heads(k_ref[...], v_ref[...], masks)
    for e in range(2):
        km_ref[e] = ks[e]
        vm_ref[e] = vs[e]
    kce, vce = _split_heads(kc_ref[0], vc_ref[0], masks)
    nq = Q_ROWS * GRID_W
    nk = KEY_ROWS * GRID_W
    for i in range(n_blk):
        qrows = slice(i * nq, (i + 1) * nq)
        k0 = min(max(Q_ROWS * i - WIN_ROWS // 2, 0), Q_ROWS * n_blk - KEY_ROWS) * GRID_W
        krows = slice(k0, k0 + nk)
        qb = q_ref[qrows, :]
        outs = []
        for e in range(2):
            s_loc = _dot_t(qb, km_ref[e, krows, :]) + bias_ref[e, i]
            s_ctx = _dot_t(qb, kce[e])
            outs.append(_softmax_pv([s_loc, s_ctx], [vm_ref[e, krows, :], vce[e]]))
        o_ref[qrows, :] = _normalise_heads(outs[0], outs[1], low)


def _lat_attn(qkv, kc, vc, rpb, batch, L):
    n_blk = L // (Q_ROWS * GRID_W)
    n_pairs = ATTN_W // LANES
    col = lambda part: pl.BlockSpec((L, LANES), lambda p, b: (b, part * n_pairs + p))
    cache = pl.BlockSpec((1, kc.shape[1], LANES), lambda p, b: (b, 0, p))
    rpb_pad = jnp.pad(rpb, ((0, 0), (0, 16 - N_DR), (0, 32 - N_DC)))
    return pl.pallas_call(
        functools.partial(_lat_attn_kernel, n_blk=n_blk),
        grid=(n_pairs, batch),
        in_specs=[col(0), col(1), col(2), cache, cache,
                  pl.BlockSpec((2, 16, 32), lambda p, b: (p, 0, 0)),
                  pl.BlockSpec((32, LANES), lambda p, b: (0, 0))],
        out_specs=pl.BlockSpec((L, LANES), lambda p, b: (b, p)),
        out_shape=jax.ShapeDtypeStruct((batch * L, ATTN_W), F32),
        scratch_shapes=[pltpu.VMEM((2, L, LANES), BF16), pltpu.VMEM((2, L, LANES), BF16),
                        pltpu.VMEM((2, 16, GRID_W, LANES), F32),
                        pltpu.VMEM((2, n_blk, Q_ROWS * GRID_W, KEY_ROWS * GRID_W), F32)],
        compiler_params=_cparams(("arbitrary", "arbitrary")),
        name="lat_attn",
    )(qkv, qkv, qkv, kc, vc, rpb_pad, jnp.asarray(_rpb_expand_table()))


SUBLANES = 8


def _row_neighbours(x, L):
    n, c = x.shape
    t, tl = n // SUBLANES, L // SUBLANES
    x3 = x.reshape(t, SUBLANES, c)
    sub = lax.broadcasted_iota(jnp.int32, (1, SUBLANES, 1), 1)
    zero = jnp.zeros((1, SUBLANES, c), x.dtype)
    down = pltpu.roll(x3, 1, 1)
    up = pltpu.roll(x3, SUBLANES - 1, 1)
    down_prev = jnp.concatenate([p for s in range(0, t, tl) for p in (zero, down[s:s + tl - 1])], axis=0)
    up_next = jnp.concatenate([p for s in range(0, t, tl) for p in (up[s + 1:s + tl], zero)], axis=0)
    above = jnp.where(sub == 0, down_prev, down).reshape(n, c)
    below = jnp.where(sub == SUBLANES - 1, up_next, up).reshape(n, c)
    return above, below


def _dwconv3(x, w_ref, b_ref, L):
    above, below = _row_neighbours(x, L)
    return above * w_ref[0:1, :] + x * w_ref[1:2, :] + below * w_ref[2:3, :] + b_ref[...]


def _hyena_kernel(h1_ref, h2_ref, hv_ref, w1_ref, w2_ref, wv_ref, b1_ref, b2_ref, bv_ref,
                  kf_ref, fb_ref, a_ref, ai_ref, o_ref, *, L):
    row0 = lax.broadcasted_iota(jnp.int32, (L, 1), 0) == 0

    def long_conv(u, order):
        ub = u.astype(BF16)
        ur = _dot(a_ref[0:L, :], ub)
        ui = _dot(a_ref[L:2 * L, :], ub)
        kr = kf_ref[order, 0:L, :]
        ki = kf_ref[order, L:2 * L, :]
        yr = jnp.where(row0, ur * kr, ur * kr - ui * ki)
        yi = jnp.where(row0, ui * ki, ur * ki + ui * kr)
        y = _dot(ai_ref[:, 0:L], yr.astype(BF16)) + _dot(ai_ref[:, L:2 * L], yi.astype(BF16))
        return y + u * fb_ref[order:order + 1, :]

    def one_sequence(rows):
        v = _dwconv3(hv_ref[rows, :], wv_ref, bv_ref, L)
        z = _dwconv3(h1_ref[rows, :], w1_ref, b1_ref, L) * long_conv(v, 0)
        o_ref[rows, :] = _dwconv3(h2_ref[rows, :], w2_ref, b2_ref, L) * long_conv(z, 1)

    n_seq = h1_ref.shape[0] // L
    if n_seq == 1:
        one_sequence(slice(0, L))
    else:
        def seq_body(s, carry):
            one_sequence(pl.ds(pl.multiple_of(s * L, L), L))
            return carry

        lax.fori_loop(0, n_seq, seq_body, 0)


def _hyena(hy, conv_w, conv_b, kf, filt_bias, a_fwd, a_inv, L):
    n = hy.shape[0]
    part = lambda k, rows: pl.BlockSpec((rows, HY_CH), lambda i: (i if rows == ROW_BLOCK else 0, k))
    once = lambda shape: pl.BlockSpec(shape, lambda i: (0,) * len(shape), pipeline_mode=pl.Buffered(1))
    return pl.pallas_call(
        functools.partial(_hyena_kernel, L=L),
        grid=(n // ROW_BLOCK,),
        in_specs=[part(0, ROW_BLOCK), part(1, ROW_BLOCK), part(2, ROW_BLOCK),
                  part(0, 3), part(1, 3), part(2, 3),
                  part(0, 1), part(1, 1), part(2, 1),
                  once((2, 2 * L, HY_CH)), once((2, HY_CH)), once((2 * L, L)), once((L, 2 * L))],
        out_specs=pl.BlockSpec((ROW_BLOCK, HY_CH), lambda i: (i, 0)),
        out_shape=jax.ShapeDtypeStruct((n, HY_CH), F32),
        compiler_params=_cparams(("arbitrary",)),
        name=f"hyena_{L}",
    )(hy, hy, hy, conv_w, conv_w, conv_w, conv_b, conv_b, conv_b, kf, filt_bias, a_fwd, a_inv)


def _tail_kernel(*refs, L, mod_base, blocks_per_mod, halo):
    rows_refs, rest = refs[:9 if halo else 3], refs[9 if halo else 3:]
    (mods_ref, gg_ref, wo_ref, n2_ref, fg_ref, wu_ref, cw_ref, cb_ref, wd_ref, o_ref,
     h2_s, acc_s, ma_s, mb_s) = rest
    rows = o_ref.shape[0]
    i = pl.program_id(0)
    r = mod_base + i // blocks_per_mod
    mod = lambda k: mods_ref[k, pl.ds(r, 1), :]

    if halo:
        x_ref, a_ref, hy_ref, xp_ref, ap_ref, hp_ref, xn_ref, an_ref, hn_ref = rows_refs
        cat = lambda p, m, n: jnp.concatenate([p[...], m[...], n[...]], axis=0)
        x, a, hy = cat(xp_ref, x_ref, xn_ref), cat(ap_ref, a_ref, an_ref), cat(hp_ref, hy_ref, hn_ref)
    else:
        x, a, hy = (ref[...] for ref in rows_refs)
    gg = gg_ref[...]
    merged = jnp.concatenate([_rms(a) * gg[:, :ATTN_W], _rms(hy) * gg[:, ATTN_W:]], axis=-1).astype(BF16)
    x1 = x + mod(2) * _dot(merged, wo_ref[...])
    h2_s[...] = (_rms(x1) * n2_ref[...] * (1.0 + mod(4)) + mod(3)).astype(BF16)
    o_ref[...] = x1[halo:halo + rows]

    ext = rows + 2 * halo
    base = TAIL_PAD + halo
    n_tiles, seq_tiles = rows // SUBLANES, L // SUBLANES
    sub = lax.broadcasted_iota(jnp.int32, (1, SUBLANES, 1), 1)
    zero_tile = jnp.zeros((1, SUBLANES, FF_TILE), F32)
    starts_seq = (i * rows) & (L - 1) == 0
    ends_seq = ((i + 1) * rows) & (L - 1) == 0

    def up_proj(j, dst):
        for e in range(2):
            cols = pl.ds(pl.multiple_of(e * D_FF + j * FF_TILE, FF_TILE), FF_TILE)
            dst[e, TAIL_PAD:TAIL_PAD + ext, :] = _dot(h2_s[...], wu_ref[:, cols])

    def conv(src, e, j):
        t = src[e, base - SUBLANES:base + rows + SUBLANES, :].reshape(n_tiles + 2, SUBLANES, FF_TILE)
        down = pltpu.roll(t, 1, 1)
        up = pltpu.roll(t, SUBLANES - 1, 1)
        if L >= rows:
            above = [jnp.where(starts_seq, zero_tile, down[0:1]), down[1:n_tiles]]
            below = [up[2:n_tiles + 1], jnp.where(ends_seq, zero_tile, up[n_tiles + 1:n_tiles + 2])]
        else:
            above, below = [], []
            for s in range(0, n_tiles, seq_tiles):
                above += [down[0:1] if s == 0 else zero_tile, down[s + 1:s + seq_tiles]]
                last_seq = s + seq_tiles == n_tiles
                below += [up[s + 2:s + seq_tiles + 1], up[n_tiles + 1:n_tiles + 2] if last_seq else zero_tile]
        above = jnp.where(sub == 0, jnp.concatenate(above, axis=0), down[1:n_tiles + 1])
        below = jnp.where(sub == SUBLANES - 1, jnp.concatenate(below, axis=0), up[1:n_tiles + 1])
        w = cw_ref[e, j]
        return above * w[0:1] + t[1:n_tiles + 1] * w[1:2] + below * w[2:3] + cb_ref[e, j]

    def down_proj(j, src):
        gate = conv(src, 0, j)
        act = gate / (1.0 + jnp.exp(-gate)) * conv(src, 1, j)
        return _dot(act.reshape(rows, FF_TILE).astype(BF16), wd_ref[j])

    for buf in (ma_s, mb_s):
        for e in range(2):
            buf[e, 0:TAIL_PAD, :] = jnp.zeros((TAIL_PAD, FF_TILE), F32)
            buf[e, TAIL_PAD + ext:, :] = jnp.zeros((TAIL_PAD, FF_TILE), F32)
    up_proj(0, ma_s)
    acc_s[...] = jnp.zeros_like(acc_s)

    def two_tiles(k, carry):
        j = 2 * k
        up_proj(j + 1, mb_s)
        acc_s[...] += down_proj(j, ma_s)
        up_proj(j + 2, ma_s)
        acc_s[...] += down_proj(j + 1, mb_s)
        return carry

    lax.fori_loop(0, (N_FF_TILES - 1) // 2, two_tiles, 0)
    x2 = o_ref[...] + mod(5) * (acc_s[...] + down_proj(N_FF_TILES - 1, ma_s))
    o_ref[...] = _rms(x2) * fg_ref[...]


def _tail(x, a, hyo, mods, grp_g, w_out, n2_g, final_g, w_up, conv_w, conv_b, w_down, *, L, mod_base,
          blocks_per_mod):
    n = x.shape[0]
    halo = TAIL_HALO if L > TAIL_ROWS else 0
    row = lambda w: pl.BlockSpec((TAIL_ROWS, w), lambda i: (i, 0))
    per_halo = TAIL_ROWS // TAIL_HALO
    prev = lambda w: pl.BlockSpec((TAIL_HALO, w), lambda i: (jnp.maximum(i * per_halo - 1, 0), 0))
    nxt = lambda w: pl.BlockSpec((TAIL_HALO, w), lambda i: (jnp.minimum((i + 1) * per_halo, n // TAIL_HALO - 1), 0))
    once = lambda shape: pl.BlockSpec(shape, lambda i: (0,) * len(shape), pipeline_mode=pl.Buffered(1))
    widths = (D_MODEL, ATTN_W, HY_CH)
    row_specs = [row(w) for w in widths]
    row_args = [x, a, hyo]
    if halo:
        row_specs += [prev(w) for w in widths] + [nxt(w) for w in widths]
        row_args += [x, a, hyo, x, a, hyo]
    ext = TAIL_ROWS + 2 * halo
    return pl.pallas_call(
        functools.partial(_tail_kernel, L=L, mod_base=mod_base, blocks_per_mod=blocks_per_mod, halo=halo),
        grid=(n // TAIL_ROWS,),
        in_specs=row_specs + [
            once((6, 8, D_MODEL)), once((1, D_MODEL)), once((D_MODEL, D_MODEL)), once((1, D_MODEL)),
            once((1, D_MODEL)),
            once((D_MODEL, 2 * D_FF)),
            once((2, N_FF_TILES, 3, FF_TILE)), once((2, N_FF_TILES, 1, FF_TILE)),
            once((N_FF_TILES, FF_TILE, D_MODEL))],
        out_specs=row(D_MODEL),
        out_shape=jax.ShapeDtypeStruct((n, D_MODEL), F32),
        scratch_shapes=[pltpu.VMEM((ext, D_MODEL), BF16), pltpu.VMEM((TAIL_ROWS, D_MODEL), F32),
                        pltpu.VMEM((2, ext + 2 * TAIL_PAD, FF_TILE), F32),
                        pltpu.VMEM((2, ext + 2 * TAIL_PAD, FF_TILE), F32)],
        compiler_params=_cparams(("arbitrary",)),
        name=f"tail_{L}",
    )(*row_args, mods, grp_g, w_out, n2_g, final_g, w_up, conv_w, conv_b, w_down)


def kernel(x_prompt, x_sample, cache_ctx_k, cache_ctx_v, c, c_ctx, w_ada, b_ada, norm1_g, w_in, rpb,
           hy_conv_w, hy_conv_b, filt_w1, filt_b1, filt_w2, filt_b2, filt_w3, filt_b3, filt_freq,
           filt_bias, grp_norm_g, w_out, norm2_g, w_up, ffn_conv_w, ffn_conv_b, w_down, final_g):
    assert w_ada.shape[0] == 1, "single-layer trunk"
    bc, lc, _ = x_prompt.shape
    bl, ll, _ = x_sample.shape
    past = cache_ctx_k.shape[3]

    cvec = jnp.concatenate([c_ctx[None], c, jnp.zeros((8 - 1 - bl, D_MODEL), F32)], axis=0)
    mods = _mods(cvec, w_ada[0], b_ada[0])

    w_in_b = w_in[0].astype(BF16)
    w_out_b = w_out[0].astype(BF16)
    w_down_t = w_down[0].astype(BF16).reshape(N_FF_TILES, FF_TILE, D_MODEL)
    conv_w_t = ffn_conv_w[0].reshape(3, 2, N_FF_TILES, FF_TILE).transpose(1, 2, 0, 3)
    conv_b_t = ffn_conv_b[0].reshape(2, N_FF_TILES, 1, FF_TILE)
    g1 = norm1_g[0][None]
    shared_tail = (grp_norm_g[0][None], w_out_b, norm2_g[0][None], final_g[None], w_up[0].astype(BF16),
                   conv_w_t, conv_b_t, w_down_t)
    filt = (filt_w1[0], filt_b1[0], filt_w2[0], filt_b2[0], filt_w3[0], filt_b3[0], filt_freq[0])

    def tables(L):
        fwd, inv = _dft_tables(L)
        return jnp.asarray(fwd).astype(BF16), jnp.asarray(inv).astype(BF16)

    xc = x_prompt.reshape(bc * lc, D_MODEL)
    a_fwd, a_inv = tables(lc)
    kf = _filter_spectrum(lc, a_fwd, *filt)
    qkv, hy, state_k, state_v = _proj(xc, mods, g1, w_in_b, mod_base=0, blocks_per_mod=bc * lc // PROJ_ROWS,
                                      L=lc, emit_state=True)
    att = _ctx_attn(qkv, lc)
    hyo = _hyena(hy, hy_conv_w[0], hy_conv_b[0][None], kf, filt_bias[0], a_fwd, a_inv, lc)
    y_prompt = _tail(xc, att, hyo, mods, *shared_tail, L=lc, mod_base=0,
                     blocks_per_mod=bc * lc // TAIL_ROWS).reshape(bc, lc, D_MODEL)

    xs = x_sample.reshape(bl * ll, D_MODEL)
    a_fwd, a_inv = tables(ll)
    kf = _filter_spectrum(ll, a_fwd, *filt)
    qkv, hy = _proj(xs, mods, g1, w_in_b, mod_base=1, blocks_per_mod=ll // PROJ_ROWS, L=ll, emit_state=False)
    lanes_major = lambda t: t[:, 0].transpose(0, 2, 1, 3).reshape(bl, past, ATTN_W).astype(BF16)
    att = _lat_attn(qkv, lanes_major(cache_ctx_k), lanes_major(cache_ctx_v), rpb[0], bl, ll)
    hyo = _hyena(hy, hy_conv_w[0], hy_conv_b[0][None], kf, filt_bias[0], a_fwd, a_inv, ll)
    y_sample = _tail(xs, att, hyo, mods, *shared_tail, L=ll, mod_base=1,
                     blocks_per_mod=ll // TAIL_ROWS).reshape(bl, ll, D_MODEL)

    return (y_prompt, y_sample, state_k, state_v)
```

```python
import functools
import math

import numpy as np
import jax
import jax.numpy as jnp
from jax import lax
from jax.experimental import pallas as pl
from jax.experimental.pallas import tpu as pltpu

F32 = jnp.float32
BF16 = jnp.bfloat16
HIGHEST = lax.Precision.HIGHEST

D_MODEL = 1024
N_HEADS = 8
HEAD_DIM = 64
ATTN_W = N_HEADS * HEAD_DIM
HY_CH = D_MODEL - ATTN_W
IN_COLS = 3 * ATTN_W + 3 * HY_CH
D_FF = 2816
GRID_W = 64
WIN_ROWS = 8
WIN_COLS = 16
FILT_FREQS = 8
FILT_HID = 64
FILT_EMB_PAD = 32
DECAY_TARGET = 1e-2
MAX_DECAY = math.log(DECAY_TARGET) / 0.3
MIN_DECAY = math.log(DECAY_TARGET) / 1.5
EPS = 1e-6
NEG = -1e30

LANES = 128
ROW_BLOCK = 1024
PROJ_ROWS = 512
TAIL_ROWS = 512
TAIL_HALO = 16
TAIL_PAD = 8
FF_TILE = 256
N_FF_TILES = D_FF // FF_TILE
assert N_FF_TILES % 2 == 1 and N_FF_TILES >= 3, "the ConvFFN pipeline peels an odd tile count"
Q_ROWS = 4
KEY_ROWS = 12
VMEM_LIMIT = 56 * 1024 * 1024


def _cparams(sem):
    return pltpu.CompilerParams(dimension_semantics=sem, vmem_limit_bytes=VMEM_LIMIT)


def _rms(x):
    return x * lax.rsqrt(jnp.mean(x * x, axis=-1, keepdims=True) + EPS)


def _dot(a, b):
    return jnp.dot(a, b, preferred_element_type=F32)


def _dot3(a, b):
    ah = a.astype(BF16)
    bh = b.astype(BF16)
    al = (a - ah.astype(F32)).astype(BF16)
    bl = (b - bh.astype(F32)).astype(BF16)
    return _dot(ah, bh) + _dot(al, bh) + _dot(ah, bl)


def _dot_t(a, b):
    return lax.dot_general(a, b, (((1,), (1,)), ((), ())), preferred_element_type=F32)


def _mods_kernel(c_ref, w_ref, b_ref, o_ref):
    cv = c_ref[...]
    s = cv / (1.0 + jnp.exp(-cv))
    o_ref[0] = _dot3(s, w_ref[...]) + b_ref[0]


def _mods(cvec, w_ada, b_ada):
    return pl.pallas_call(
        _mods_kernel,
        grid=(6,),
        in_specs=[
            pl.BlockSpec((8, D_MODEL), lambda j: (0, 0)),
            pl.BlockSpec((D_MODEL, D_MODEL), lambda j: (0, j)),
            pl.BlockSpec((1, 1, D_MODEL), lambda j: (j, 0, 0)),
        ],
        out_specs=pl.BlockSpec((1, 8, D_MODEL), lambda j: (j, 0, 0)),
        out_shape=jax.ShapeDtypeStruct((6, 8, D_MODEL), F32),
        compiler_params=_cparams(("arbitrary",)),
        name="mods",
    )(cvec, w_ada, b_ada.reshape(6, 1, D_MODEL))


def _dft_tables(L):
    f = np.arange(L, dtype=np.int64)[:, None]
    t = np.arange(L, dtype=np.int64)[None, :]
    ang = np.pi * ((f * t) % (2 * L)).astype(np.float64) / L
    top = np.cos(ang)
    bot = -np.sin(ang)
    bot[0, :] = np.where(np.arange(L) % 2 == 0, 1.0, -1.0)
    fwd = np.concatenate([top, bot], axis=0)
    w = np.full((2 * L,), 1.0 / L)
    w[0] = w[L] = 0.5 / L
    inv = (fwd * w[:, None]).T
    return fwd.astype(np.float32), inv.astype(np.float32)


def _filter_consts(L):
    t = np.arange(L, dtype=np.float64) / L
    fr = np.arange(1, FILT_FREQS + 1, dtype=np.float64)
    ang = 2.0 * math.pi * fr[:, None] * t[None, :]
    z = np.zeros((FILT_EMB_PAD, L), np.float64)
    z[0] = t
    z[1:1 + FILT_FREQS] = np.cos(ang)
    z[1 + FILT_FREQS:1 + 2 * FILT_FREQS] = np.sin(ang)
    deltas = np.abs(np.linspace(MIN_DECAY, MAX_DECAY, HY_CH))
    decay = np.exp(-t[:, None] * deltas[None, :])
    return z.astype(np.float32), decay.astype(np.float32)


def _filter_kernel(z_ref, w1_ref, b1_ref, w2_ref, b2_ref, w3_ref, b3_ref, fr_ref, dec_ref,
                   a_ref, o_ref, h_s, *, L):
    @pl.when(pl.program_id(0) == 0)
    def _():
        fr = fr_ref[...]
        h = jnp.sin(fr * (jnp.dot(w1_ref[...], z_ref[...], precision=HIGHEST,
                                  preferred_element_type=F32) + b1_ref[...]))
        h_s[...] = jnp.sin(fr * (jnp.dot(w2_ref[...], h, precision=HIGHEST,
                                         preferred_element_type=F32) + b2_ref[...]))

    tdot = lambda a, b: lax.dot_general(a, b, (((0,), (0,)), ((), ())), preferred_element_type=F32)
    hid, w3 = h_s[...], w3_ref[...]
    hh, wh = hid.astype(BF16), w3.astype(BF16)
    hl, wl = (hid - hh.astype(F32)).astype(BF16), (w3 - wh.astype(F32)).astype(BF16)
    h = tdot(hh, wh) + tdot(hl, wh) + tdot(hh, wl) + b3_ref[...]
    dec = dec_ref[...]
    row0 = lax.broadcasted_iota(jnp.int32, (L, 1), 0) == 0
    fwd = h[:, :HY_CH] * dec
    bwd = jnp.where(row0, 0.0, h[:, HY_CH:] * dec)
    gp = (fwd + bwd).astype(BF16)
    gm = (fwd - bwd).astype(BF16)
    top = _dot(a_ref[0:L, :], gp)
    bot = _dot(a_ref[L:2 * L, :], gm)
    nyq = _dot(a_ref[L:L + 16, :], gp)[0:1]
    o_ref[0, 0:L, :] = top
    o_ref[0, L:2 * L, :] = jnp.where(row0, nyq, bot)


def _filter_spectrum(L, a_fwd, w1, b1, w2, b2, w3, b3, freq):
    z, decay = _filter_consts(L)
    full = lambda shape: pl.BlockSpec(shape, lambda o: (0,) * len(shape))
    col = lambda v: v[:, None]
    return pl.pallas_call(
        functools.partial(_filter_kernel, L=L),
        grid=(2,),
        in_specs=[
            full((FILT_EMB_PAD, L)),
            full((FILT_HID, FILT_EMB_PAD)),
            full((FILT_HID, 1)),
            full((FILT_HID, FILT_HID)),
            full((FILT_HID, 1)),
            pl.BlockSpec((FILT_HID, 2 * HY_CH), lambda o: (0, o)),
            pl.BlockSpec((1, 2 * HY_CH), lambda o: (0, o)),
            full((FILT_HID, 1)),
            full((L, HY_CH)),
            full((2 * L, L)),
        ],
        out_specs=pl.BlockSpec((1, 2 * L, HY_CH), lambda o: (o, 0, 0)),
        out_shape=jax.ShapeDtypeStruct((2, 2 * L, HY_CH), F32),
        scratch_shapes=[pltpu.VMEM((FILT_HID, L), F32)],
        compiler_params=_cparams(("arbitrary",)),
        name=f"filter_{L}",
    )(jnp.asarray(z), jnp.pad(w1, ((0, FILT_EMB_PAD - w1.shape[0]), (0, 0))).T, col(b1), w2.T, col(b2),
      w3, b3[None], col(freq), jnp.asarray(decay), a_fwd)


def _store_heads_major(x, dst_ref, L):
    for s in range(x.shape[0] // L):
        for h in range(N_HEADS):
            dst_ref[s, 0, h] = x[s * L:(s + 1) * L, h * HEAD_DIM:(h + 1) * HEAD_DIM]


def _proj_kernel(x_ref, mods_ref, g_ref, w_ref, qkv_ref, hy_ref, *state, mod_base, blocks_per_mod, L):
    r = mod_base + pl.program_id(0) // blocks_per_mod
    sh1 = mods_ref[0, pl.ds(r, 1), :]
    sc1 = mods_ref[1, pl.ds(r, 1), :]
    h = (_rms(x_ref[...]) * g_ref[...] * (1.0 + sc1) + sh1).astype(BF16)
    q = _dot(h, w_ref[:, 0:ATTN_W])
    qkv_ref[:, 0:ATTN_W] = (q * HEAD_DIM ** -0.5).astype(BF16)
    k = _dot(h, w_ref[:, ATTN_W:2 * ATTN_W])
    qkv_ref[:, ATTN_W:2 * ATTN_W] = k.astype(BF16)
    v = _dot(h, w_ref[:, 2 * ATTN_W:3 * ATTN_W])
    qkv_ref[:, 2 * ATTN_W:3 * ATTN_W] = v.astype(BF16)
    if state:
        ks_ref, vs_ref = state
        _store_heads_major(k, ks_ref, L)
        _store_heads_major(v, vs_ref, L)
    hy_ref[...] = _dot(h, w_ref[:, 3 * ATTN_W:])


def _proj(x, mods, g, w_in, *, mod_base, blocks_per_mod, L, emit_state):
    n = x.shape[0]
    row = lambda w: pl.BlockSpec((PROJ_ROWS, w), lambda i: (i, 0))
    out_specs = [row(3 * ATTN_W), row(3 * HY_CH)]
    out_shape = [jax.ShapeDtypeStruct((n, 3 * ATTN_W), BF16), jax.ShapeDtypeStruct((n, 3 * HY_CH), F32)]
    if emit_state:
        seqs = PROJ_ROWS // L
        state = pl.BlockSpec((seqs, 1, N_HEADS, L, HEAD_DIM), lambda i: (i, 0, 0, 0, 0))
        out_specs += [state, state]
        out_shape += [jax.ShapeDtypeStruct((n // L, 1, N_HEADS, L, HEAD_DIM), F32)] * 2
    return pl.pallas_call(
        functools.partial(_proj_kernel, mod_base=mod_base, blocks_per_mod=blocks_per_mod, L=L),
        grid=(n // PROJ_ROWS,),
        in_specs=[
            row(D_MODEL),
            pl.BlockSpec((6, 8, D_MODEL), lambda i: (0, 0, 0)),
            pl.BlockSpec((1, D_MODEL), lambda i: (0, 0)),
            pl.BlockSpec((D_MODEL, IN_COLS), lambda i: (0, 0)),
        ],
        out_specs=out_specs,
        out_shape=out_shape,
        compiler_params=_cparams(("arbitrary",)),
        name="proj_ctx" if emit_state else "proj_lat",
    )(x, mods, g, w_in)


def _softmax_pv(scores, values):
    mx = functools.reduce(jnp.maximum, [jnp.max(s, axis=-1, keepdims=True) for s in scores])
    return functools.reduce(jnp.add, [_dot(jnp.exp(s - mx).astype(BF16), v) for s, v in zip(scores, values)])


def _normalise_heads(o0, o1, low):
    den = pltpu.roll(jnp.where(low, o1, o0), HEAD_DIM, 1)
    return jnp.where(low, o0, o1) * (1.0 / den)


def _split_heads(k, v, masks):
    ks = [jnp.where(m, k, jnp.zeros_like(k)) for m in masks]
    vs = [jnp.where(m, v, jnp.ones_like(v)) for m in masks]
    return ks, vs


def _ctx_attn_kernel(qkv_ref, o_ref, *, n_seq, L):
    low = lax.broadcasted_iota(jnp.int32, (1, LANES), 1) < HEAD_DIM
    masks = (low, jnp.logical_not(low))

    def seq_body(s, carry):
        rows = pl.ds(pl.multiple_of(s * L, L), L)
        for p in range(ATTN_W // LANES):
            cols = lambda part: slice(part * ATTN_W + p * LANES, part * ATTN_W + (p + 1) * LANES)
            qb = qkv_ref[rows, cols(0)]
            kb = qkv_ref[rows, cols(1)]
            vb = qkv_ref[rows, cols(2)]
            outs = []
            for m in masks:
                s_ = _dot_t(qb, jnp.where(m, kb, jnp.zeros_like(kb)))
                p_ = jnp.exp(s_ - jnp.max(s_, axis=-1, keepdims=True))
                outs.append(_dot(p_.astype(BF16), vb) * (1.0 / jnp.sum(p_, axis=-1, keepdims=True)))
            o_ref[rows, p * LANES:(p + 1) * LANES] = jnp.where(low, outs[0], outs[1])
        return carry

    lax.fori_loop(0, n_seq, seq_body, 0)


def _ctx_attn(qkv, L):
    n = qkv.shape[0]
    return pl.pallas_call(
        functools.partial(_ctx_attn_kernel, n_seq=ROW_BLOCK // L, L=L),
        grid=(n // ROW_BLOCK,),
        in_specs=[pl.BlockSpec((ROW_BLOCK, 3 * ATTN_W), lambda i: (i, 0))],
        out_specs=pl.BlockSpec((ROW_BLOCK, ATTN_W), lambda i: (i, 0)),
        out_shape=jax.ShapeDtypeStruct((n, ATTN_W), F32),
        compiler_params=_cparams(("arbitrary",)),
        name="ctx_attn",
    )(qkv)


N_DR = 2 * WIN_ROWS - 1
N_DC = 2 * WIN_COLS - 1


def _rpb_expand_table():
    d = np.clip(np.arange(LANES) - GRID_W, -(WIN_COLS - 1), WIN_COLS - 1) + (WIN_COLS - 1)
    t = np.zeros((32, LANES), np.float32)
    t[d, np.arange(LANES)] = 1.0
    return t


def _bias_tile_index(n_rows):
    n_blk = n_rows // Q_ROWS
    idx = np.full((n_blk, Q_ROWS, KEY_ROWS), N_DR, np.int32)
    for i in range(n_blk):
        ks = min(max(Q_ROWS * i - WIN_ROWS // 2, 0), n_rows - KEY_ROWS)
        for a in range(Q_ROWS):
            r = Q_ROWS * i + a
            rs = min(max(r - WIN_ROWS // 2, 0), n_rows - WIN_ROWS)
            for j in range(KEY_ROWS):
                if rs <= ks + j < rs + WIN_ROWS:
                    idx[i, a, j] = ks + j - r + (WIN_ROWS - 1)
            assert (idx[i, a] != N_DR).sum() == WIN_ROWS, "key window must cover the whole band"
    return idx


def _build_bias(rpb_ref, ext_ref, tw_ref, bias_ref, n_blk):
    shape = (GRID_W, LANES)
    lane = lax.broadcasted_iota(jnp.int32, shape, 1)
    qc = lax.broadcasted_iota(jnp.int32, shape, 0)
    kc = lane & (GRID_W - 1)
    low = lane < GRID_W
    col_start = jnp.clip(qc - WIN_COLS // 2, 0, GRID_W - WIN_COLS)
    col_in = (kc >= col_start) & (kc < col_start + WIN_COLS)
    idx = _bias_tile_index(Q_ROWS * n_blk)
    for e in range(2):
        g = jnp.dot(rpb_ref[e], ext_ref[...], precision=HIGHEST, preferred_element_type=F32)
        for dr in range(N_DR):
            t = pltpu.roll(jnp.broadcast_to(g[dr:dr + 1, :], shape), GRID_W, 1, stride=1, stride_axis=0)
            t = jnp.where(low, t, pltpu.roll(t, GRID_W, 1))
            tw_ref[e, dr] = jnp.where(col_in, t, NEG)
        tw_ref[e, N_DR] = jnp.full(shape, NEG, F32)
        for i in range(n_blk):
            for a in range(Q_ROWS):
                for jt in range(KEY_ROWS // 2):
                    tile = jnp.where(low, tw_ref[e, int(idx[i, a, 2 * jt])], tw_ref[e, int(idx[i, a, 2 * jt + 1])])
                    bias_ref[e, i, a * GRID_W:(a + 1) * GRID_W, jt * LANES:(jt + 1) * LANES] = tile


def _lat_attn_kernel(q_ref, k_ref, v_ref, kc_ref, vc_ref, rpb_ref, ext_ref, o_ref, km_ref, vm_ref, tw_ref,
                     bias_ref, *, n_blk):
    @pl.when(pl.program_id(1) == 0)
    def _():
        _build_bias(rpb_ref, ext_ref, tw_ref, bias_ref, n_blk)

    low = lax.broadcasted_iota(jnp.int32, (1, LANES), 1) < HEAD_DIM
    masks = (low, jnp.logical_not(low))
    ks, vs = _split_heads(k_ref[...], v_ref[...], masks)
    for e in range(2):
        km_ref[e] = ks[e]
        vm_ref[e] = vs[e]
    kce, vce = _split_heads(kc_ref[0], vc_ref[0], masks)
    nq = Q_ROWS * GRID_W
    nk = KEY_ROWS * GRID_W
    for i in range(n_blk):
        qrows = slice(i * nq, (i + 1) * nq)
        k0 = min(max(Q_ROWS * i - WIN_ROWS // 2, 0), Q_ROWS * n_blk - KEY_ROWS) * GRID_W
        krows = slice(k0, k0 + nk)
        qb = q_ref[qrows, :]
        outs = []
        for e in range(2):
            s_loc = _dot_t(qb, km_ref[e, krows, :]) + bias_ref[e, i]
            s_ctx = _dot_t(qb, kce[e])
            outs.append(_softmax_pv([s_loc, s_ctx], [vm_ref[e, krows, :], vce[e]]))
        o_ref[qrows, :] = _normalise_heads(outs[0], outs[1], low)


def _lat_attn(qkv, kc, vc, rpb, batch, L):
    n_blk = L // (Q_ROWS * GRID_W)
    n_pairs = ATTN_W // LANES
    col = lambda part: pl.BlockSpec((L, LANES), lambda p, b: (b, part * n_pairs + p))
    cache = pl.BlockSpec((1, kc.shape[1], LANES), lambda p, b: (b, 0, p))
    rpb_pad = jnp.pad(rpb, ((0, 0), (0, 16 - N_DR), (0, 32 - N_DC)))
    return pl.pallas_call(
        functools.partial(_lat_attn_kernel, n_blk=n_blk),
        grid=(n_pairs, batch),
        in_specs=[col(0), col(1), col(2), cache, cache,
                  pl.BlockSpec((2, 16, 32), lambda p, b: (p, 0, 0)),
                  pl.BlockSpec((32, LANES), lambda p, b: (0, 0))],
        out_specs=pl.BlockSpec((L, LANES), lambda p, b: (b, p)),
        out_shape=jax.ShapeDtypeStruct((batch * L, ATTN_W), F32),
        scratch_shapes=[pltpu.VMEM((2, L, LANES), BF16), pltpu.VMEM((2, L, LANES), BF16),
                        pltpu.VMEM((2, 16, GRID_W, LANES), F32),
                        pltpu.VMEM((2, n_blk, Q_ROWS * GRID_W, KEY_ROWS * GRID_W), F32)],
        compiler_params=_cparams(("arbitrary", "arbitrary")),
        name="lat_attn",
    )(qkv, qkv, qkv, kc, vc, rpb_pad, jnp.asarray(_rpb_expand_table()))


SUBLANES = 8


def _row_neighbours(x, L):
    n, c = x.shape
    t, tl = n // SUBLANES, L // SUBLANES
    x3 = x.reshape(t, SUBLANES, c)
    sub = lax.broadcasted_iota(jnp.int32, (1, SUBLANES, 1), 1)
    zero = jnp.zeros((1, SUBLANES, c), x.dtype)
    down = pltpu.roll(x3, 1, 1)
    up = pltpu.roll(x3, SUBLANES - 1, 1)
    down_prev = jnp.concatenate([p for s in range(0, t, tl) for p in (zero, down[s:s + tl - 1])], axis=0)
    up_next = jnp.concatenate([p for s in range(0, t, tl) for p in (up[s + 1:s + tl], zero)], axis=0)
    above = jnp.where(sub == 0, down_prev, down).reshape(n, c)
    below = jnp.where(sub == SUBLANES - 1, up_next, up).reshape(n, c)
    return above, below


def _dwconv3(x, w_ref, b_ref, L):
    above, below = _row_neighbours(x, L)
    return above * w_ref[0:1, :] + x * w_ref[1:2, :] + below * w_ref[2:3, :] + b_ref[...]


def _hyena_kernel(h1_ref, h2_ref, hv_ref, w1_ref, w2_ref, wv_ref, b1_ref, b2_ref, bv_ref,
                  kf_ref, fb_ref, a_ref, ai_ref, o_ref, *, L):
    row0 = lax.broadcasted_iota(jnp.int32, (L, 1), 0) == 0

    def long_conv(u, order):
        ub = u.astype(BF16)
        ur = _dot(a_ref[0:L, :], ub)
        ui = _dot(a_ref[L:2 * L, :], ub)
        kr = kf_ref[order, 0:L, :]
        ki = kf_ref[order, L:2 * L, :]
        yr = jnp.where(row0, ur * kr, ur * kr - ui * ki)
        yi = jnp.where(row0, ui * ki, ur * ki + ui * kr)
        y = _dot(ai_ref[:, 0:L], yr.astype(BF16)) + _dot(ai_ref[:, L:2 * L], yi.astype(BF16))
        return y + u * fb_ref[order:order + 1, :]

    def one_sequence(rows):
        v = _dwconv3(hv_ref[rows, :], wv_ref, bv_ref, L)
        z = _dwconv3(h1_ref[rows, :], w1_ref, b1_ref, L) * long_conv(v, 0)
        o_ref[rows, :] = _dwconv3(h2_ref[rows, :], w2_ref, b2_ref, L) * long_conv(z, 1)

    n_seq = h1_ref.shape[0] // L
    if n_seq == 1:
        one_sequence(slice(0, L))
    else:
        def seq_body(s, carry):
            one_sequence(pl.ds(pl.multiple_of(s * L, L), L))
            return carry

        lax.fori_loop(0, n_seq, seq_body, 0)


def _hyena(hy, conv_w, conv_b, kf, filt_bias, a_fwd, a_inv, L):
    n = hy.shape[0]
    part = lambda k, rows: pl.BlockSpec((rows, HY_CH), lambda i: (i if rows == ROW_BLOCK else 0, k))
    once = lambda shape: pl.BlockSpec(shape, lambda i: (0,) * len(shape), pipeline_mode=pl.Buffered(1))
    return pl.pallas_call(
        functools.partial(_hyena_kernel, L=L),
        grid=(n // ROW_BLOCK,),
        in_specs=[part(0, ROW_BLOCK), part(1, ROW_BLOCK), part(2, ROW_BLOCK),
                  part(0, 3), part(1, 3), part(2, 3),
                  part(0, 1), part(1, 1), part(2, 1),
                  once((2, 2 * L, HY_CH)), once((2, HY_CH)), once((2 * L, L)), once((L, 2 * L))],
        out_specs=pl.BlockSpec((ROW_BLOCK, HY_CH), lambda i: (i, 0)),
        out_shape=jax.ShapeDtypeStruct((n, HY_CH), F32),
        compiler_params=_cparams(("arbitrary",)),
        name=f"hyena_{L}",
    )(hy, hy, hy, conv_w, conv_w, conv_w, conv_b, conv_b, conv_b, kf, filt_bias, a_fwd, a_inv)


def _tail_kernel(*refs, L, mod_base, blocks_per_mod, halo, n_blocks):
    rows_refs, rest = refs[:9 if halo else 3], refs[9 if halo else 3:]
    (mods_ref, gg_ref, wo_ref, n2_ref, fg_ref, wu_ref, cw_ref, cb_ref, wd_ref, o_ref,
     h2_s, x1_s, acc_s, ma_s, mb_s, mc_s) = rest
    rows = o_ref.shape[0]
    i = pl.program_id(0)
    ext = rows + 2 * halo
    base = TAIL_PAD + halo
    n_tiles, seq_tiles = rows // SUBLANES, L // SUBLANES
    last_tile = N_FF_TILES - 1
    sub = lax.broadcasted_iota(jnp.int32, (1, SUBLANES, 1), 1)
    zero_tile = jnp.zeros((1, SUBLANES, FF_TILE), F32)

    def mod(k, blk):
        return mods_ref[k, pl.ds(mod_base + blk // blocks_per_mod, 1), :]

    def up_proj(j, dst):
        for e in range(2):
            cols = pl.ds(pl.multiple_of(e * D_FF + j * FF_TILE, FF_TILE), FF_TILE)
            dst[e, TAIL_PAD:TAIL_PAD + ext, :] = _dot(h2_s[...], wu_ref[:, cols])

    def conv(src, e, j, blk):
        t = src[e, base - SUBLANES:base + rows + SUBLANES, :].reshape(n_tiles + 2, SUBLANES, FF_TILE)
        down = pltpu.roll(t, 1, 1)
        up = pltpu.roll(t, SUBLANES - 1, 1)
        if L >= rows:
            starts_seq = (blk * rows) & (L - 1) == 0
            ends_seq = ((blk + 1) * rows) & (L - 1) == 0
            above = [jnp.where(starts_seq, zero_tile, down[0:1]), down[1:n_tiles]]
            below = [up[2:n_tiles + 1], jnp.where(ends_seq, zero_tile, up[n_tiles + 1:n_tiles + 2])]
        else:
            above, below = [], []
            for s in range(0, n_tiles, seq_tiles):
                above += [down[0:1] if s == 0 else zero_tile, down[s + 1:s + seq_tiles]]
                last_seq = s + seq_tiles == n_tiles
                below += [up[s + 2:s + seq_tiles + 1], up[n_tiles + 1:n_tiles + 2] if last_seq else zero_tile]
        above = jnp.where(sub == 0, jnp.concatenate(above, axis=0), down[1:n_tiles + 1])
        below = jnp.where(sub == SUBLANES - 1, jnp.concatenate(below, axis=0), up[1:n_tiles + 1])
        w = cw_ref[e, j]
        return above * w[0:1] + t[1:n_tiles + 1] * w[1:2] + below * w[2:3] + cb_ref[e, j]

    def down_proj(j, src, blk):
        gate = conv(src, 0, j, blk)
        act = gate / (1.0 + jnp.exp(-gate)) * conv(src, 1, j, blk)
        return _dot(act.reshape(rows, FF_TILE).astype(BF16), wd_ref[j])

    def start_block(blk):
        if halo:
            x_ref, a_ref, hy_ref, xp_ref, ap_ref, hp_ref, xn_ref, an_ref, hn_ref = rows_refs
            cat = lambda p, m, n: jnp.concatenate([p[...], m[...], n[...]], axis=0)
            x, a, hy = cat(xp_ref, x_ref, xn_ref), cat(ap_ref, a_ref, an_ref), cat(hp_ref, hy_ref, hn_ref)
        else:
            x, a, hy = (ref[...] for ref in rows_refs)
        gg = gg_ref[...]
        merged = jnp.concatenate([_rms(a) * gg[:, :ATTN_W], _rms(hy) * gg[:, ATTN_W:]], axis=-1).astype(BF16)
        x1 = x + mod(2, blk) * _dot(merged, wo_ref[...])
        h2_s[...] = (_rms(x1) * n2_ref[...] * (1.0 + mod(4, blk)) + mod(3, blk)).astype(BF16)
        x1_s[...] = x1[halo:halo + rows]
        up_proj(0, mc_s)
        up_proj(1, mb_s)
        acc_s[...] = down_proj(0, mc_s, blk)
        up_proj(2, ma_s)
        acc_s[...] += down_proj(1, mb_s, blk)

    def finish_block(blk):
        x2 = x1_s[...] + mod(5, blk) * (acc_s[...] + down_proj(last_tile, ma_s, blk))
        o_ref[...] = _rms(x2) * fg_ref[...]

    @pl.when(i == 0)
    def _():
        for buf in (ma_s, mb_s, mc_s):
            for e in range(2):
                buf[e, 0:TAIL_PAD, :] = jnp.zeros((TAIL_PAD, FF_TILE), F32)
                buf[e, TAIL_PAD + ext:, :] = jnp.zeros((TAIL_PAD, FF_TILE), F32)
        start_block(i)

    @pl.when(jnp.logical_and(i > 0, i < n_blocks))
    def _():
        finish_block(i - 1)
        start_block(i)

    @pl.when(i == n_blocks)
    def _():
        finish_block(i - 1)

    @pl.when(i < n_blocks)
    def _():
        def two_tiles(k, carry):
            j = 2 * k
            up_proj(j + 1, mb_s)
            acc_s[...] += down_proj(j, ma_s, i)
            up_proj(j + 2, ma_s)
            acc_s[...] += down_proj(j + 1, mb_s, i)
            return carry

        lax.fori_loop(1, (N_FF_TILES - 1) // 2, two_tiles, 0)


def _tail(x, a, hyo, mods, grp_g, w_out, n2_g, final_g, w_up, conv_w, conv_b, w_down, *, L, mod_base,
          blocks_per_mod):
    n = x.shape[0]
    n_blocks = n // TAIL_ROWS
    halo = TAIL_HALO if L > TAIL_ROWS else 0
    cur = lambda i: jnp.minimum(i, n_blocks - 1)
    row = lambda w: pl.BlockSpec((TAIL_ROWS, w), lambda i: (cur(i), 0))
    per_halo = TAIL_ROWS // TAIL_HALO
    prev = lambda w: pl.BlockSpec((TAIL_HALO, w), lambda i: (jnp.maximum(cur(i) * per_halo - 1, 0), 0))
    nxt = lambda w: pl.BlockSpec(
        (TAIL_HALO, w), lambda i: (jnp.minimum((cur(i) + 1) * per_halo, n // TAIL_HALO - 1), 0))
    once = lambda shape: pl.BlockSpec(shape, lambda i: (0,) * len(shape), pipeline_mode=pl.Buffered(1))
    widths = (D_MODEL, ATTN_W, HY_CH)
    row_specs = [row(w) for w in widths]
    row_args = [x, a, hyo]
    if halo:
        row_specs += [prev(w) for w in widths] + [nxt(w) for w in widths]
        row_args += [x, a, hyo, x, a, hyo]
    ext = TAIL_ROWS + 2 * halo
    staging = pltpu.VMEM((2, ext + 2 * TAIL_PAD, FF_TILE), F32)
    return pl.pallas_call(
        functools.partial(_tail_kernel, L=L, mod_base=mod_base, blocks_per_mod=blocks_per_mod, halo=halo,
                          n_blocks=n_blocks),
        grid=(n_blocks + 1,),
        in_specs=row_specs + [
            once((6, 8, D_MODEL)), once((1, D_MODEL)), once((D_MODEL, D_MODEL)), once((1, D_MODEL)),
            once((1, D_MODEL)),
            once((D_MODEL, 2 * D_FF)),
            once((2, N_FF_TILES, 3, FF_TILE)), once((2, N_FF_TILES, 1, FF_TILE)),
            once((N_FF_TILES, FF_TILE, D_MODEL))],
        out_specs=pl.BlockSpec((TAIL_ROWS, D_MODEL), lambda i: (jnp.maximum(i - 1, 0), 0)),
        out_shape=jax.ShapeDtypeStruct((n, D_MODEL), F32),
        scratch_shapes=[pltpu.VMEM((ext, D_MODEL), BF16), pltpu.VMEM((TAIL_ROWS, D_MODEL), F32),
                        pltpu.VMEM((TAIL_ROWS, D_MODEL), F32), staging, staging, staging],
        compiler_params=_cparams(("arbitrary",)),
        name=f"tail_{L}",
    )(*row_args, mods, grp_g, w_out, n2_g, final_g, w_up, conv_w, conv_b, w_down)


def kernel(x_prompt, x_sample, cache_ctx_k, cache_ctx_v, c, c_ctx, w_ada, b_ada, norm1_g, w_in, rpb,
           hy_conv_w, hy_conv_b, filt_w1, filt_b1, filt_w2, filt_b2, filt_w3, filt_b3, filt_freq,
           filt_bias, grp_norm_g, w_out, norm2_g, w_up, ffn_conv_w, ffn_conv_b, w_down, final_g):
    assert w_ada.shape[0] == 1, "single-layer trunk"
    bc, lc, _ = x_prompt.shape
    bl, ll, _ = x_sample.shape
    past = cache_ctx_k.shape[3]

    cvec = jnp.concatenate([c_ctx[None], c, jnp.zeros((8 - 1 - bl, D_MODEL), F32)], axis=0)
    mods = _mods(cvec, w_ada[0], b_ada[0])

    w_in_b = w_in[0].astype(BF16)
    w_out_b = w_out[0].astype(BF16)
    w_down_t = w_down[0].astype(BF16).reshape(N_FF_TILES, FF_TILE, D_MODEL)
    conv_w_t = ffn_conv_w[0].reshape(3, 2, N_FF_TILES, FF_TILE).transpose(1, 2, 0, 3)
    conv_b_t = ffn_conv_b[0].reshape(2, N_FF_TILES, 1, FF_TILE)
    g1 = norm1_g[0][None]
    shared_tail = (grp_norm_g[0][None], w_out_b, norm2_g[0][None], final_g[None], w_up[0].astype(BF16),
                   conv_w_t, conv_b_t, w_down_t)
    filt = (filt_w1[0], filt_b1[0], filt_w2[0], filt_b2[0], filt_w3[0], filt_b3[0], filt_freq[0])

    def tables(L):
        fwd, inv = _dft_tables(L)
        return jnp.asarray(fwd).astype(BF16), jnp.asarray(inv).astype(BF16)

    xc = x_prompt.reshape(bc * lc, D_MODEL)
    a_fwd, a_inv = tables(lc)
    kf = _filter_spectrum(lc, a_fwd, *filt)
    qkv, hy, state_k, state_v = _proj(xc, mods, g1, w_in_b, mod_base=0, blocks_per_mod=bc * lc // PROJ_ROWS,
                                      L=lc, emit_state=True)
    att = _ctx_attn(qkv, lc)
    hyo = _hyena(hy, hy_conv_w[0], hy_conv_b[0][None], kf, filt_bias[0], a_fwd, a_inv, lc)
    y_prompt = _tail(xc, att, hyo, mods, *shared_tail, L=lc, mod_base=0,
                     blocks_per_mod=bc * lc // TAIL_ROWS).reshape(bc, lc, D_MODEL)

    xs = x_sample.reshape(bl * ll, D_MODEL)
    a_fwd, a_inv = tables(ll)
    kf = _filter_spectrum(ll, a_fwd, *filt)
    qkv, hy = _proj(xs, mods, g1, w_in_b, mod_base=1, blocks_per_mod=ll // PROJ_ROWS, L=ll, emit_state=False)
    lanes_major = lambda t: t[:, 0].transpose(0, 2, 1, 3).reshape(bl, past, ATTN_W).astype(BF16)
    att = _lat_attn(qkv, lanes_major(cache_ctx_k), lanes_major(cache_ctx_v), rpb[0], bl, ll)
    hyo = _hyena(hy, hy_conv_w[0], hy_conv_b[0][None], kf, filt_bias[0], a_fwd, a_inv, ll)
    y_sample = _tail(xs, att, hyo, mods, *shared_tail, L=ll, mod_base=1,
                     blocks_per_mod=ll // TAIL_ROWS).reshape(bl, ll, D_MODEL)

    return (y_prompt, y_sample, state_k, state_v)
```

```python
import functools
import math

import numpy as np
import jax
import jax.numpy as jnp
from jax import lax
from jax.experimental import pallas as pl
from jax.experimental.pallas import tpu as pltpu

F32 = jnp.float32
BF16 = jnp.bfloat16
HIGHEST = lax.Precision.HIGHEST

D_MODEL = 1024
N_HEADS = 8
HEAD_DIM = 64
ATTN_W = N_HEADS * HEAD_DIM
HY_CH = D_MODEL - ATTN_W
IN_COLS = 3 * ATTN_W + 3 * HY_CH
D_FF = 2816
GRID_W = 64
WIN_ROWS = 8
WIN_COLS = 16
FILT_FREQS = 8
FILT_HID = 64
FILT_EMB_PAD = 32
DECAY_TARGET = 1e-2
MAX_DECAY = math.log(DECAY_TARGET) / 0.3
MIN_DECAY = math.log(DECAY_TARGET) / 1.5
EPS = 1e-6
NEG = -1e30

LANES = 128
ROW_BLOCK = 1024
PROJ_ROWS = 512
TAIL_ROWS = 512
TAIL_HALO = 16
TAIL_PAD = 8
FF_TILE = 256
N_FF_TILES = D_FF // FF_TILE
assert N_FF_TILES % 2 == 1 and N_FF_TILES >= 3, "the ConvFFN pipeline peels an odd tile count"
Q_ROWS = 4
KEY_ROWS = 12
VMEM_LIMIT = 56 * 1024 * 1024


def _cparams(sem):
    return pltpu.CompilerParams(dimension_semantics=sem, vmem_limit_bytes=VMEM_LIMIT)


def _rms(x):
    return x * lax.rsqrt(jnp.mean(x * x, axis=-1, keepdims=True) + EPS)


def _dot(a, b):
    return jnp.dot(a, b, preferred_element_type=F32)


def _dot3(a, b):
    ah = a.astype(BF16)
    bh = b.astype(BF16)
    al = (a - ah.astype(F32)).astype(BF16)
    bl = (b - bh.astype(F32)).astype(BF16)
    return _dot(ah, bh) + _dot(al, bh) + _dot(ah, bl)


def _dot_t(a, b):
    return lax.dot_general(a, b, (((1,), (1,)), ((), ())), preferred_element_type=F32)


def _with_casts(body, n_in, n_out, n_cast):
    def kernel(*refs):
        a, b, c = n_in + n_cast, n_in + n_cast + n_out, n_in + 2 * n_cast + n_out
        for src, dst in zip(refs[n_in:a], refs[b:c]):
            dst[...] = src[...].astype(BF16)
        body(*refs[:n_in], *refs[a:b], *refs[c:])

    return kernel


def _cast_specs(casts, n_steps):
    specs = [pl.BlockSpec((w.shape[0] // n_steps, w.shape[1]), lambda i: (i, 0)) for w in casts]
    return specs, [jax.ShapeDtypeStruct(w.shape, BF16) for w in casts]


def _mods_kernel(c_ref, w_ref, b_ref, o_ref):
    cv = c_ref[...]
    s = cv / (1.0 + jnp.exp(-cv))
    o_ref[0] = _dot3(s, w_ref[...]) + b_ref[0]


def _mods(cvec, w_ada, b_ada):
    return pl.pallas_call(
        _mods_kernel,
        grid=(6,),
        in_specs=[
            pl.BlockSpec((8, D_MODEL), lambda j: (0, 0)),
            pl.BlockSpec((D_MODEL, D_MODEL), lambda j: (0, j)),
            pl.BlockSpec((1, 1, D_MODEL), lambda j: (j, 0, 0)),
        ],
        out_specs=pl.BlockSpec((1, 8, D_MODEL), lambda j: (j, 0, 0)),
        out_shape=jax.ShapeDtypeStruct((6, 8, D_MODEL), F32),
        compiler_params=_cparams(("arbitrary",)),
        name="mods",
    )(cvec, w_ada, b_ada.reshape(6, 1, D_MODEL))


def _dft_tables(L):
    f = np.arange(L, dtype=np.int64)[:, None]
    t = np.arange(L, dtype=np.int64)[None, :]
    ang = np.pi * ((f * t) % (2 * L)).astype(np.float64) / L
    top = np.cos(ang)
    bot = -np.sin(ang)
    bot[0, :] = np.where(np.arange(L) % 2 == 0, 1.0, -1.0)
    fwd = np.concatenate([top, bot], axis=0)
    w = np.full((2 * L,), 1.0 / L)
    w[0] = w[L] = 0.5 / L
    inv = (fwd * w[:, None]).T
    return fwd.astype(np.float32), inv.astype(np.float32)


def _filter_consts(L):
    t = np.arange(L, dtype=np.float64) / L
    fr = np.arange(1, FILT_FREQS + 1, dtype=np.float64)
    ang = 2.0 * math.pi * fr[:, None] * t[None, :]
    z = np.zeros((FILT_EMB_PAD, L), np.float64)
    z[0] = t
    z[1:1 + FILT_FREQS] = np.cos(ang)
    z[1 + FILT_FREQS:1 + 2 * FILT_FREQS] = np.sin(ang)
    deltas = np.abs(np.linspace(MIN_DECAY, MAX_DECAY, HY_CH))
    decay = np.exp(-t[:, None] * deltas[None, :])
    return z.astype(np.float32), decay.astype(np.float32)


def _filter_kernel(z_ref, w1_ref, b1_ref, w2_ref, b2_ref, w3_ref, b3_ref, fr_ref, dec_ref,
                   a_ref, o_ref, h_s, *, L):
    @pl.when(pl.program_id(0) == 0)
    def _():
        fr = fr_ref[...]
        h = jnp.sin(fr * (jnp.dot(w1_ref[...], z_ref[...], precision=HIGHEST,
                                  preferred_element_type=F32) + b1_ref[...]))
        h_s[...] = jnp.sin(fr * (jnp.dot(w2_ref[...], h, precision=HIGHEST,
                                         preferred_element_type=F32) + b2_ref[...]))

    tdot = lambda a, b: lax.dot_general(a, b, (((0,), (0,)), ((), ())), preferred_element_type=F32)
    hid, w3 = h_s[...], w3_ref[...]
    hh, wh = hid.astype(BF16), w3.astype(BF16)
    hl, wl = (hid - hh.astype(F32)).astype(BF16), (w3 - wh.astype(F32)).astype(BF16)
    h = tdot(hh, wh) + tdot(hl, wh) + tdot(hh, wl) + b3_ref[...]
    dec = dec_ref[...]
    row0 = lax.broadcasted_iota(jnp.int32, (L, 1), 0) == 0
    fwd = h[:, :HY_CH] * dec
    bwd = jnp.where(row0, 0.0, h[:, HY_CH:] * dec)
    gp = (fwd + bwd).astype(BF16)
    gm = (fwd - bwd).astype(BF16)
    top = _dot(a_ref[0:L, :], gp)
    bot = _dot(a_ref[L:2 * L, :], gm)
    nyq = _dot(a_ref[L:L + 16, :], gp)[0:1]
    o_ref[0, 0:L, :] = top
    o_ref[0, L:2 * L, :] = jnp.where(row0, nyq, bot)


def _filter_spectrum(L, a_fwd, w1, b1, w2, b2, w3, b3, freq, casts=()):
    z, decay = _filter_consts(L)
    full = lambda shape: pl.BlockSpec(shape, lambda o: (0,) * len(shape))
    col = lambda v: v[:, None]
    cast_specs, cast_shapes = _cast_specs(casts, 2)
    return pl.pallas_call(
        _with_casts(functools.partial(_filter_kernel, L=L), 10, 1, len(casts)),
        grid=(2,),
        in_specs=[
            full((FILT_EMB_PAD, L)),
            full((FILT_HID, FILT_EMB_PAD)),
            full((FILT_HID, 1)),
            full((FILT_HID, FILT_HID)),
            full((FILT_HID, 1)),
            pl.BlockSpec((FILT_HID, 2 * HY_CH), lambda o: (0, o)),
            pl.BlockSpec((1, 2 * HY_CH), lambda o: (0, o)),
            full((FILT_HID, 1)),
            full((L, HY_CH)),
            full((2 * L, L)),
        ] + cast_specs,
        out_specs=[pl.BlockSpec((1, 2 * L, HY_CH), lambda o: (o, 0, 0))] + cast_specs,
        out_shape=[jax.ShapeDtypeStruct((2, 2 * L, HY_CH), F32)] + cast_shapes,
        scratch_shapes=[pltpu.VMEM((FILT_HID, L), F32)],
        compiler_params=_cparams(("arbitrary",)),
        name=f"filter_{L}",
    )(jnp.asarray(z), jnp.pad(w1, ((0, FILT_EMB_PAD - w1.shape[0]), (0, 0))).T, col(b1), w2.T, col(b2),
      w3, b3[None], col(freq), jnp.asarray(decay), a_fwd, *casts)


def _store_heads_major(x, dst_ref, L):
    for s in range(x.shape[0] // L):
        for h in range(N_HEADS):
            dst_ref[s, 0, h] = x[s * L:(s + 1) * L, h * HEAD_DIM:(h + 1) * HEAD_DIM]


def _proj_kernel(x_ref, mods_ref, g_ref, w_ref, qkv_ref, hy_ref, *state, mod_base, blocks_per_mod, L):
    r = mod_base + pl.program_id(0) // blocks_per_mod
    sh1 = mods_ref[0, pl.ds(r, 1), :]
    sc1 = mods_ref[1, pl.ds(r, 1), :]
    h = (_rms(x_ref[...]) * g_ref[...] * (1.0 + sc1) + sh1).astype(BF16)
    q = _dot(h, w_ref[:, 0:ATTN_W])
    qkv_ref[:, 0:ATTN_W] = (q * HEAD_DIM ** -0.5).astype(BF16)
    k = _dot(h, w_ref[:, ATTN_W:2 * ATTN_W])
    qkv_ref[:, ATTN_W:2 * ATTN_W] = k.astype(BF16)
    v = _dot(h, w_ref[:, 2 * ATTN_W:3 * ATTN_W])
    qkv_ref[:, 2 * ATTN_W:3 * ATTN_W] = v.astype(BF16)
    if state:
        ks_ref, vs_ref = state
        _store_heads_major(k, ks_ref, L)
        _store_heads_major(v, vs_ref, L)
    hy_ref[...] = _dot(h, w_ref[:, 3 * ATTN_W:])


def _proj(x, mods, g, w_in, *, mod_base, blocks_per_mod, L, emit_state):
    n = x.shape[0]
    row = lambda w: pl.BlockSpec((PROJ_ROWS, w), lambda i: (i, 0))
    out_specs = [row(3 * ATTN_W), row(3 * HY_CH)]
    out_shape = [jax.ShapeDtypeStruct((n, 3 * ATTN_W), BF16), jax.ShapeDtypeStruct((n, 3 * HY_CH), F32)]
    if emit_state:
        seqs = PROJ_ROWS // L
        state = pl.BlockSpec((seqs, 1, N_HEADS, L, HEAD_DIM), lambda i: (i, 0, 0, 0, 0))
        out_specs += [state, state]
        out_shape += [jax.ShapeDtypeStruct((n // L, 1, N_HEADS, L, HEAD_DIM), F32)] * 2
    return pl.pallas_call(
        functools.partial(_proj_kernel, mod_base=mod_base, blocks_per_mod=blocks_per_mod, L=L),
        grid=(n // PROJ_ROWS,),
        in_specs=[
            row(D_MODEL),
            pl.BlockSpec((6, 8, D_MODEL), lambda i: (0, 0, 0)),
            pl.BlockSpec((1, D_MODEL), lambda i: (0, 0)),
            pl.BlockSpec((D_MODEL, IN_COLS), lambda i: (0, 0)),
        ],
        out_specs=out_specs,
        out_shape=out_shape,
        compiler_params=_cparams(("arbitrary",)),
        name="proj_ctx" if emit_state else "proj_lat",
    )(x, mods, g, w_in)


def _softmax_pv(scores, values):
    mx = functools.reduce(jnp.maximum, [jnp.max(s, axis=-1, keepdims=True) for s in scores])
    return functools.reduce(jnp.add, [_dot(jnp.exp(s - mx).astype(BF16), v) for s, v in zip(scores, values)])


def _normalise_heads(o0, o1, low):
    den = pltpu.roll(jnp.where(low, o1, o0), HEAD_DIM, 1)
    return jnp.where(low, o0, o1) * (1.0 / den)


def _split_heads(k, v, masks):
    ks = [jnp.where(m, k, jnp.zeros_like(k)) for m in masks]
    vs = [jnp.where(m, v, jnp.ones_like(v)) for m in masks]
    return ks, vs


def _ctx_attn_kernel(qkv_ref, o_ref, *, n_seq, L):
    low = lax.broadcasted_iota(jnp.int32, (1, LANES), 1) < HEAD_DIM
    masks = (low, jnp.logical_not(low))

    def seq_body(s, carry):
        rows = pl.ds(pl.multiple_of(s * L, L), L)
        for p in range(ATTN_W // LANES):
            cols = lambda part: slice(part * ATTN_W + p * LANES, part * ATTN_W + (p + 1) * LANES)
            qb = qkv_ref[rows, cols(0)]
            kb = qkv_ref[rows, cols(1)]
            vb = qkv_ref[rows, cols(2)]
            outs = []
            for m in masks:
                s_ = _dot_t(qb, jnp.where(m, kb, jnp.zeros_like(kb)))
                p_ = jnp.exp(s_ - jnp.max(s_, axis=-1, keepdims=True))
                outs.append(_dot(p_.astype(BF16), vb) * (1.0 / jnp.sum(p_, axis=-1, keepdims=True)))
            o_ref[rows, p * LANES:(p + 1) * LANES] = jnp.where(low, outs[0], outs[1])
        return carry

    lax.fori_loop(0, n_seq, seq_body, 0)


def _ctx_attn(qkv, L, casts=()):
    n = qkv.shape[0]
    cast_specs, cast_shapes = _cast_specs(casts, n // ROW_BLOCK)
    return pl.pallas_call(
        _with_casts(functools.partial(_ctx_attn_kernel, n_seq=ROW_BLOCK // L, L=L), 1, 1, len(casts)),
        grid=(n // ROW_BLOCK,),
        in_specs=[pl.BlockSpec((ROW_BLOCK, 3 * ATTN_W), lambda i: (i, 0))] + cast_specs,
        out_specs=[pl.BlockSpec((ROW_BLOCK, ATTN_W), lambda i: (i, 0))] + cast_specs,
        out_shape=[jax.ShapeDtypeStruct((n, ATTN_W), F32)] + cast_shapes,
        compiler_params=_cparams(("arbitrary",)),
        name="ctx_attn",
    )(qkv, *casts)


N_DR = 2 * WIN_ROWS - 1
N_DC = 2 * WIN_COLS - 1


def _rpb_expand_table():
    d = np.clip(np.arange(LANES) - GRID_W, -(WIN_COLS - 1), WIN_COLS - 1) + (WIN_COLS - 1)
    t = np.zeros((32, LANES), np.float32)
    t[d, np.arange(LANES)] = 1.0
    return t


def _bias_tile_index(n_rows):
    n_blk = n_rows // Q_ROWS
    idx = np.full((n_blk, Q_ROWS, KEY_ROWS), N_DR, np.int32)
    for i in range(n_blk):
        ks = min(max(Q_ROWS * i - WIN_ROWS // 2, 0), n_rows - KEY_ROWS)
        for a in range(Q_ROWS):
            r = Q_ROWS * i + a
            rs = min(max(r - WIN_ROWS // 2, 0), n_rows - WIN_ROWS)
            for j in range(KEY_ROWS):
                if rs <= ks + j < rs + WIN_ROWS:
                    idx[i, a, j] = ks + j - r + (WIN_ROWS - 1)
            assert (idx[i, a] != N_DR).sum() == WIN_ROWS, "key window must cover the whole band"
    return idx


def _build_bias(rpb_ref, ext_ref, tw_ref, bias_ref, n_blk):
    shape = (GRID_W, LANES)
    lane = lax.broadcasted_iota(jnp.int32, shape, 1)
    qc = lax.broadcasted_iota(jnp.int32, shape, 0)
    kc = lane & (GRID_W - 1)
    low = lane < GRID_W
    col_start = jnp.clip(qc - WIN_COLS // 2, 0, GRID_W - WIN_COLS)
    col_in = (kc >= col_start) & (kc < col_start + WIN_COLS)
    idx = _bias_tile_index(Q_ROWS * n_blk)
    for e in range(2):
        g = jnp.dot(rpb_ref[e], ext_ref[...], precision=HIGHEST, preferred_element_type=F32)
        for dr in range(N_DR):
            t = pltpu.roll(jnp.broadcast_to(g[dr:dr + 1, :], shape), GRID_W, 1, stride=1, stride_axis=0)
            t = jnp.where(low, t, pltpu.roll(t, GRID_W, 1))
            tw_ref[e, dr] = jnp.where(col_in, t, NEG)
        tw_ref[e, N_DR] = jnp.full(shape, NEG, F32)
        for i in range(n_blk):
            for a in range(Q_ROWS):
                for jt in range(KEY_ROWS // 2):
                    tile = jnp.where(low, tw_ref[e, int(idx[i, a, 2 * jt])], tw_ref[e, int(idx[i, a, 2 * jt + 1])])
                    bias_ref[e, i, a * GRID_W:(a + 1) * GRID_W, jt * LANES:(jt + 1) * LANES] = tile


def _lat_attn_kernel(q_ref, k_ref, v_ref, kc_ref, vc_ref, rpb_ref, ext_ref, o_ref, km_ref, vm_ref, tw_ref,
                     bias_ref, *, n_blk):
    @pl.when(pl.program_id(1) == 0)
    def _():
        _build_bias(rpb_ref, ext_ref, tw_ref, bias_ref, n_blk)

    low = lax.broadcasted_iota(jnp.int32, (1, LANES), 1) < HEAD_DIM
    masks = (low, jnp.logical_not(low))
    ks, vs = _split_heads(k_ref[...], v_ref[...], masks)
    for e in range(2):
        km_ref[e] = ks[e]
        vm_ref[e] = vs[e]
    kce, vce = _split_heads(kc_ref[0], vc_ref[0], masks)
    nq = Q_ROWS * GRID_W
    nk = KEY_ROWS * GRID_W
    for i in range(n_blk):
        qrows = slice(i * nq, (i + 1) * nq)
        k0 = min(max(Q_ROWS * i - WIN_ROWS // 2, 0), Q_ROWS * n_blk - KEY_ROWS) * GRID_W
        krows = slice(k0, k0 + nk)
        qb = q_ref[qrows, :]
        outs = []
        for e in range(2):
            s_loc = _dot_t(qb, km_ref[e, krows, :]) + bias_ref[e, i]
            s_ctx = _dot_t(qb, kce[e])
            outs.append(_softmax_pv([s_loc, s_ctx], [vm_ref[e, krows, :], vce[e]]))
        o_ref[qrows, :] = _normalise_heads(outs[0], outs[1], low)


def _lat_attn(qkv, kc, vc, rpb, batch, L):
    n_blk = L // (Q_ROWS * GRID_W)
    n_pairs = ATTN_W // LANES
    col = lambda part: pl.BlockSpec((L, LANES), lambda p, b: (b, part * n_pairs + p))
    cache = pl.BlockSpec((1, kc.shape[1], LANES), lambda p, b: (b, 0, p))
    rpb_pad = jnp.pad(rpb, ((0, 0), (0, 16 - N_DR), (0, 32 - N_DC)))
    return pl.pallas_call(
        functools.partial(_lat_attn_kernel, n_blk=n_blk),
        grid=(n_pairs, batch),
        in_specs=[col(0), col(1), col(2), cache, cache,
                  pl.BlockSpec((2, 16, 32), lambda p, b: (p, 0, 0)),
                  pl.BlockSpec((32, LANES), lambda p, b: (0, 0))],
        out_specs=pl.BlockSpec((L, LANES), lambda p, b: (b, p)),
        out_shape=jax.ShapeDtypeStruct((batch * L, ATTN_W), F32),
        scratch_shapes=[pltpu.VMEM((2, L, LANES), BF16), pltpu.VMEM((2, L, LANES), BF16),
                        pltpu.VMEM((2, 16, GRID_W, LANES), F32),
                        pltpu.VMEM((2, n_blk, Q_ROWS * GRID_W, KEY_ROWS * GRID_W), F32)],
        compiler_params=_cparams(("arbitrary", "arbitrary")),
        name="lat_attn",
    )(qkv, qkv, qkv, kc, vc, rpb_pad, jnp.asarray(_rpb_expand_table()))


SUBLANES = 8


def _row_neighbours(x, L):
    n, c = x.shape
    t, tl = n // SUBLANES, L // SUBLANES
    x3 = x.reshape(t, SUBLANES, c)
    sub = lax.broadcasted_iota(jnp.int32, (1, SUBLANES, 1), 1)
    zero = jnp.zeros((1, SUBLANES, c), x.dtype)
    down = pltpu.roll(x3, 1, 1)
    up = pltpu.roll(x3, SUBLANES - 1, 1)
    down_prev = jnp.concatenate([p for s in range(0, t, tl) for p in (zero, down[s:s + tl - 1])], axis=0)
    up_next = jnp.concatenate([p for s in range(0, t, tl) for p in (up[s + 1:s + tl], zero)], axis=0)
    above = jnp.where(sub == 0, down_prev, down).reshape(n, c)
    below = jnp.where(sub == SUBLANES - 1, up_next, up).reshape(n, c)
    return above, below


def _dwconv3(x, w_ref, b_ref, L):
    above, below = _row_neighbours(x, L)
    return above * w_ref[0:1, :] + x * w_ref[1:2, :] + below * w_ref[2:3, :] + b_ref[...]


def _hyena_kernel(h1_ref, h2_ref, hv_ref, w1_ref, w2_ref, wv_ref, b1_ref, b2_ref, bv_ref,
                  kf_ref, fb_ref, a_ref, ai_ref, o_ref, *, L):
    row0 = lax.broadcasted_iota(jnp.int32, (L, 1), 0) == 0

    def long_conv(u, order):
        ub = u.astype(BF16)
        ur = _dot(a_ref[0:L, :], ub)
        ui = _dot(a_ref[L:2 * L, :], ub)
        kr = kf_ref[order, 0:L, :]
        ki = kf_ref[order, L:2 * L, :]
        yr = jnp.where(row0, ur * kr, ur * kr - ui * ki)
        yi = jnp.where(row0, ui * ki, ur * ki + ui * kr)
        y = _dot(ai_ref[:, 0:L], yr.astype(BF16)) + _dot(ai_ref[:, L:2 * L], yi.astype(BF16))
        return y + u * fb_ref[order:order + 1, :]

    def one_sequence(rows):
        v = _dwconv3(hv_ref[rows, :], wv_ref, bv_ref, L)
        z = _dwconv3(h1_ref[rows, :], w1_ref, b1_ref, L) * long_conv(v, 0)
        o_ref[rows, :] = _dwconv3(h2_ref[rows, :], w2_ref, b2_ref, L) * long_conv(z, 1)

    n_seq = h1_ref.shape[0] // L
    if n_seq == 1:
        one_sequence(slice(0, L))
    else:
        def seq_body(s, carry):
            one_sequence(pl.ds(pl.multiple_of(s * L, L), L))
            return carry

        lax.fori_loop(0, n_seq, seq_body, 0)


def _hyena(hy, conv_w, conv_b, kf, filt_bias, a_fwd, a_inv, L, casts=()):
    n = hy.shape[0]
    cast_specs, cast_shapes = _cast_specs(casts, n // ROW_BLOCK)
    part = lambda k, rows: pl.BlockSpec((rows, HY_CH), lambda i: (i if rows == ROW_BLOCK else 0, k))
    once = lambda shape: pl.BlockSpec(shape, lambda i: (0,) * len(shape), pipeline_mode=pl.Buffered(1))
    return pl.pallas_call(
        _with_casts(functools.partial(_hyena_kernel, L=L), 13, 1, len(casts)),
        grid=(n // ROW_BLOCK,),
        in_specs=[part(0, ROW_BLOCK), part(1, ROW_BLOCK), part(2, ROW_BLOCK),
                  part(0, 3), part(1, 3), part(2, 3),
                  part(0, 1), part(1, 1), part(2, 1),
                  once((2, 2 * L, HY_CH)), once((2, HY_CH)), once((2 * L, L)), once((L, 2 * L))] + cast_specs,
        out_specs=[pl.BlockSpec((ROW_BLOCK, HY_CH), lambda i: (i, 0))] + cast_specs,
        out_shape=[jax.ShapeDtypeStruct((n, HY_CH), F32)] + cast_shapes,
        compiler_params=_cparams(("arbitrary",)),
        name=f"hyena_{L}",
    )(hy, hy, hy, conv_w, conv_w, conv_w, conv_b, conv_b, conv_b, kf, filt_bias, a_fwd, a_inv, *casts)


def _tail_kernel(*refs, L, mod_base, blocks_per_mod, halo, n_blocks):
    rows_refs, rest = refs[:9 if halo else 3], refs[9 if halo else 3:]
    (mods_ref, gg_ref, wo_ref, n2_ref, fg_ref, wu_ref, cw_ref, cb_ref, wd_ref, o_ref,
     h2_s, x1_s, acc_s, ma_s, mb_s, mc_s) = rest
    rows = o_ref.shape[0]
    i = pl.program_id(0)
    ext = rows + 2 * halo
    base = TAIL_PAD + halo
    n_tiles, seq_tiles = rows // SUBLANES, L // SUBLANES
    last_tile = N_FF_TILES - 1
    sub = lax.broadcasted_iota(jnp.int32, (1, SUBLANES, 1), 1)
    zero_tile = jnp.zeros((1, SUBLANES, FF_TILE), F32)

    def mod(k, blk):
        return mods_ref[k, pl.ds(mod_base + blk // blocks_per_mod, 1), :]

    def up_proj(j, dst):
        for e in range(2):
            cols = pl.ds(pl.multiple_of(e * D_FF + j * FF_TILE, FF_TILE), FF_TILE)
            dst[e, TAIL_PAD:TAIL_PAD + ext, :] = _dot(h2_s[...], wu_ref[:, cols])

    def conv(src, e, j, blk):
        t = src[e, base - SUBLANES:base + rows + SUBLANES, :].reshape(n_tiles + 2, SUBLANES, FF_TILE)
        down = pltpu.roll(t, 1, 1)
        up = pltpu.roll(t, SUBLANES - 1, 1)
        if L >= rows:
            starts_seq = (blk * rows) & (L - 1) == 0
            ends_seq = ((blk + 1) * rows) & (L - 1) == 0
            above = [jnp.where(starts_seq, zero_tile, down[0:1]), down[1:n_tiles]]
            below = [up[2:n_tiles + 1], jnp.where(ends_seq, zero_tile, up[n_tiles + 1:n_tiles + 2])]
        else:
            above, below = [], []
            for s in range(0, n_tiles, seq_tiles):
                above += [down[0:1] if s == 0 else zero_tile, down[s + 1:s + seq_tiles]]
                last_seq = s + seq_tiles == n_tiles
                below += [up[s + 2:s + seq_tiles + 1], up[n_tiles + 1:n_tiles + 2] if last_seq else zero_tile]
        above = jnp.where(sub == 0, jnp.concatenate(above, axis=0), down[1:n_tiles + 1])
        below = jnp.where(sub == SUBLANES - 1, jnp.concatenate(below, axis=0), up[1:n_tiles + 1])
        w = cw_ref[e, j]
        return above * w[0:1] + t[1:n_tiles + 1] * w[1:2] + below * w[2:3] + cb_ref[e, j]

    def down_proj(j, src, blk):
        gate = conv(src, 0, j, blk)
        act = gate / (1.0 + jnp.exp(-gate)) * conv(src, 1, j, blk)
        return _dot(act.reshape(rows, FF_TILE).astype(BF16), wd_ref[j])

    def start_block(blk):
        if halo:
            x_ref, a_ref, hy_ref, xp_ref, ap_ref, hp_ref, xn_ref, an_ref, hn_ref = rows_refs
            cat = lambda p, m, n: jnp.concatenate([p[...], m[...], n[...]], axis=0)
            x, a, hy = cat(xp_ref, x_ref, xn_ref), cat(ap_ref, a_ref, an_ref), cat(hp_ref, hy_ref, hn_ref)
        else:
            x, a, hy = (ref[...] for ref in rows_refs)
        gg = gg_ref[...]
        merged = jnp.concatenate([_rms(a) * gg[:, :ATTN_W], _rms(hy) * gg[:, ATTN_W:]], axis=-1).astype(BF16)
        x1 = x + mod(2, blk) * _dot(merged, wo_ref[...])
        h2_s[...] = (_rms(x1) * n2_ref[...] * (1.0 + mod(4, blk)) + mod(3, blk)).astype(BF16)
        x1_s[...] = x1[halo:halo + rows]
        up_proj(0, mc_s)
        up_proj(1, mb_s)
        acc_s[...] = down_proj(0, mc_s, blk)
        up_proj(2, ma_s)
        acc_s[...] += down_proj(1, mb_s, blk)

    def finish_block(blk):
        x2 = x1_s[...] + mod(5, blk) * (acc_s[...] + down_proj(last_tile, ma_s, blk))
        o_ref[...] = _rms(x2) * fg_ref[...]

    @pl.when(i == 0)
    def _():
        for buf in (ma_s, mb_s, mc_s):
            for e in range(2):
                buf[e, 0:TAIL_PAD, :] = jnp.zeros((TAIL_PAD, FF_TILE), F32)
                buf[e, TAIL_PAD + ext:, :] = jnp.zeros((TAIL_PAD, FF_TILE), F32)
        start_block(i)

    @pl.when(jnp.logical_and(i > 0, i < n_blocks))
    def _():
        finish_block(i - 1)
        start_block(i)

    @pl.when(i == n_blocks)
    def _():
        finish_block(i - 1)

    @pl.when(i < n_blocks)
    def _():
        def two_tiles(k, carry):
            j = 2 * k
            up_proj(j + 1, mb_s)
            acc_s[...] += down_proj(j, ma_s, i)
            up_proj(j + 2, ma_s)
            acc_s[...] += down_proj(j + 1, mb_s, i)
            return carry

        lax.fori_loop(1, (N_FF_TILES - 1) // 2, two_tiles, 0)


def _tail(x, a, hyo, mods, grp_g, w_out, n2_g, final_g, w_up, conv_w, conv_b, w_down, *, L, mod_base,
          blocks_per_mod):
    n = x.shape[0]
    n_blocks = n // TAIL_ROWS
    halo = TAIL_HALO if L > TAIL_ROWS else 0
    cur = lambda i: jnp.minimum(i, n_blocks - 1)
    row = lambda w: pl.BlockSpec((TAIL_ROWS, w), lambda i: (cur(i), 0))
    per_halo = TAIL_ROWS // TAIL_HALO
    prev = lambda w: pl.BlockSpec((TAIL_HALO, w), lambda i: (jnp.maximum(cur(i) * per_halo - 1, 0), 0))
    nxt = lambda w: pl.BlockSpec(
        (TAIL_HALO, w), lambda i: (jnp.minimum((cur(i) + 1) * per_halo, n // TAIL_HALO - 1), 0))
    once = lambda shape: pl.BlockSpec(shape, lambda i: (0,) * len(shape), pipeline_mode=pl.Buffered(1))
    widths = (D_MODEL, ATTN_W, HY_CH)
    row_specs = [row(w) for w in widths]
    row_args = [x, a, hyo]
    if halo:
        row_specs += [prev(w) for w in widths] + [nxt(w) for w in widths]
        row_args += [x, a, hyo, x, a, hyo]
    ext = TAIL_ROWS + 2 * halo
    staging = pltpu.VMEM((2, ext + 2 * TAIL_PAD, FF_TILE), F32)
    return pl.pallas_call(
        functools.partial(_tail_kernel, L=L, mod_base=mod_base, blocks_per_mod=blocks_per_mod, halo=halo,
                          n_blocks=n_blocks),
        grid=(n_blocks + 1,),
        in_specs=row_specs + [
            once((6, 8, D_MODEL)), once((1, D_MODEL)), once((D_MODEL, D_MODEL)), once((1, D_MODEL)),
            once((1, D_MODEL)),
            once((D_MODEL, 2 * D_FF)),
            once((2, N_FF_TILES, 3, FF_TILE)), once((2, N_FF_TILES, 1, FF_TILE)),
            once((N_FF_TILES, FF_TILE, D_MODEL))],
        out_specs=pl.BlockSpec((TAIL_ROWS, D_MODEL), lambda i: (jnp.maximum(i - 1, 0), 0)),
        out_shape=jax.ShapeDtypeStruct((n, D_MODEL), F32),
        scratch_shapes=[pltpu.VMEM((ext, D_MODEL), BF16), pltpu.VMEM((TAIL_ROWS, D_MODEL), F32),
                        pltpu.VMEM((TAIL_ROWS, D_MODEL), F32), staging, staging, staging],
        compiler_params=_cparams(("arbitrary",)),
        name=f"tail_{L}",
    )(*row_args, mods, grp_g, w_out, n2_g, final_g, w_up, conv_w, conv_b, w_down)


def kernel(x_prompt, x_sample, cache_ctx_k, cache_ctx_v, c, c_ctx, w_ada, b_ada, norm1_g, w_in, rpb,
           hy_conv_w, hy_conv_b, filt_w1, filt_b1, filt_w2, filt_b2, filt_w3, filt_b3, filt_freq,
           filt_bias, grp_norm_g, w_out, norm2_g, w_up, ffn_conv_w, ffn_conv_b, w_down, final_g):
    assert w_ada.shape[0] == 1, "single-layer trunk"
    bc, lc, _ = x_prompt.shape
    bl, ll, _ = x_sample.shape
    past = cache_ctx_k.shape[3]

    cvec = jnp.concatenate([c_ctx[None], c, jnp.zeros((8 - 1 - bl, D_MODEL), F32)], axis=0)
    mods = _mods(cvec, w_ada[0], b_ada[0])

    conv_w_t = ffn_conv_w[0].reshape(3, 2, N_FF_TILES, FF_TILE).transpose(1, 2, 0, 3)
    conv_b_t = ffn_conv_b[0].reshape(2, N_FF_TILES, 1, FF_TILE)
    g1 = norm1_g[0][None]
    filt = (filt_w1[0], filt_b1[0], filt_w2[0], filt_b2[0], filt_w3[0], filt_b3[0], filt_freq[0])

    def tables(L):
        fwd, inv = _dft_tables(L)
        return jnp.asarray(fwd).astype(BF16), jnp.asarray(inv).astype(BF16)

    xc = x_prompt.reshape(bc * lc, D_MODEL)
    a_fwd, a_inv = tables(lc)
    kf, w_in_b = _filter_spectrum(lc, a_fwd, *filt, casts=(w_in[0],))
    qkv, hy, state_k, state_v = _proj(xc, mods, g1, w_in_b, mod_base=0, blocks_per_mod=bc * lc // PROJ_ROWS,
                                      L=lc, emit_state=True)
    att, w_up_b = _ctx_attn(qkv, lc, casts=(w_up[0],))
    hyo, w_down_b, w_out_b = _hyena(hy, hy_conv_w[0], hy_conv_b[0][None], kf, filt_bias[0], a_fwd, a_inv, lc,
                                    casts=(w_down[0], w_out[0]))
    shared_tail = (grp_norm_g[0][None], w_out_b, norm2_g[0][None], final_g[None], w_up_b, conv_w_t, conv_b_t,
                   w_down_b.reshape(N_FF_TILES, FF_TILE, D_MODEL))
    y_prompt = _tail(xc, att, hyo, mods, *shared_tail, L=lc, mod_base=0,
                     blocks_per_mod=bc * lc // TAIL_ROWS).reshape(bc, lc, D_MODEL)

    xs = x_sample.reshape(bl * ll, D_MODEL)
    a_fwd, a_inv = tables(ll)
    kf, = _filter_spectrum(ll, a_fwd, *filt)
    qkv, hy = _proj(xs, mods, g1, w_in_b, mod_base=1, blocks_per_mod=ll // PROJ_ROWS, L=ll, emit_state=False)
    lanes_major = lambda t: t[:, 0].transpose(0, 2, 1, 3).reshape(bl, past, ATTN_W).astype(BF16)
    att = _lat_attn(qkv, lanes_major(cache_ctx_k), lanes_major(cache_ctx_v), rpb[0], bl, ll)
    hyo, = _hyena(hy, hy_conv_w[0], hy_conv_b[0][None], kf, filt_bias[0], a_fwd, a_inv, ll)
    y_sample = _tail(xs, att, hyo, mods, *shared_tail, L=ll, mod_base=1,
                     blocks_per_mod=ll // TAIL_ROWS).reshape(bl, ll, D_MODEL)

    return (y_prompt, y_sample, state_k, state_v)
```

```python
import functools
import math

import numpy as np
import jax
import jax.numpy as jnp
from jax import lax
from jax.experimental import pallas as pl
from jax.experimental.pallas import tpu as pltpu

F32 = jnp.float32
BF16 = jnp.bfloat16
HIGHEST = lax.Precision.HIGHEST

D_MODEL = 1024
N_HEADS = 8
HEAD_DIM = 64
ATTN_W = N_HEADS * HEAD_DIM
HY_CH = D_MODEL - ATTN_W
IN_COLS = 3 * ATTN_W + 3 * HY_CH
D_FF = 2816
GRID_W = 64
WIN_ROWS = 8
WIN_COLS = 16
FILT_FREQS = 8
FILT_HID = 64
FILT_EMB_PAD = 32
DECAY_TARGET = 1e-2
MAX_DECAY = math.log(DECAY_TARGET) / 0.3
MIN_DECAY = math.log(DECAY_TARGET) / 1.5
EPS = 1e-6
NEG = -1e30

LANES = 128
ROW_BLOCK = 1024
PROJ_ROWS = 512
TAIL_ROWS = 512
TAIL_HALO = 16
TAIL_PAD = 8
FF_TILE = 256
N_FF_TILES = D_FF // FF_TILE
assert N_FF_TILES % 2 == 1 and N_FF_TILES >= 3, "the ConvFFN pipeline peels an odd tile count"
Q_ROWS = 4
KEY_ROWS = 12
VMEM_LIMIT = 56 * 1024 * 1024


def _cparams(sem):
    return pltpu.CompilerParams(dimension_semantics=sem, vmem_limit_bytes=VMEM_LIMIT)


def _rms(x):
    return x * lax.rsqrt(jnp.mean(x * x, axis=-1, keepdims=True) + EPS)


def _dot(a, b):
    return jnp.dot(a, b, preferred_element_type=F32)


def _dot3(a, b):
    ah = a.astype(BF16)
    bh = b.astype(BF16)
    al = (a - ah.astype(F32)).astype(BF16)
    bl = (b - bh.astype(F32)).astype(BF16)
    return _dot(ah, bh) + _dot(al, bh) + _dot(ah, bl)


def _dot_t(a, b):
    return lax.dot_general(a, b, (((1,), (1,)), ((), ())), preferred_element_type=F32)


def _with_casts(body, n_in, n_out, n_cast):
    def kernel(*refs):
        a, b, c = n_in + n_cast, n_in + n_cast + n_out, n_in + 2 * n_cast + n_out
        for src, dst in zip(refs[n_in:a], refs[b:c]):
            dst[...] = src[...].astype(BF16)
        body(*refs[:n_in], *refs[a:b], *refs[c:])

    return kernel


def _cast_specs(casts, n_steps):
    specs = [pl.BlockSpec((w.shape[0] // n_steps, w.shape[1]), lambda i: (i, 0)) for w in casts]
    return specs, [jax.ShapeDtypeStruct(w.shape, BF16) for w in casts]


def _mods_kernel(c_ref, w_ref, b_ref, o_ref):
    cv = c_ref[...]
    s = cv / (1.0 + jnp.exp(-cv))
    o_ref[0] = _dot3(s, w_ref[...]) + b_ref[0]


def _mods(cvec, w_ada, b_ada):
    return pl.pallas_call(
        _mods_kernel,
        grid=(6,),
        in_specs=[
            pl.BlockSpec((8, D_MODEL), lambda j: (0, 0)),
            pl.BlockSpec((D_MODEL, D_MODEL), lambda j: (0, j)),
            pl.BlockSpec((1, 1, D_MODEL), lambda j: (j, 0, 0)),
        ],
        out_specs=pl.BlockSpec((1, 8, D_MODEL), lambda j: (j, 0, 0)),
        out_shape=jax.ShapeDtypeStruct((6, 8, D_MODEL), F32),
        compiler_params=_cparams(("arbitrary",)),
        name="mods",
    )(cvec, w_ada, b_ada.reshape(6, 1, D_MODEL))


def _dft_tables(L):
    f = np.arange(L, dtype=np.int64)[:, None]
    t = np.arange(L, dtype=np.int64)[None, :]
    ang = np.pi * ((f * t) % (2 * L)).astype(np.float64) / L
    top = np.cos(ang)
    bot = -np.sin(ang)
    bot[0, :] = np.where(np.arange(L) % 2 == 0, 1.0, -1.0)
    fwd = np.concatenate([top, bot], axis=0)
    w = np.full((2 * L,), 1.0 / L)
    w[0] = w[L] = 0.5 / L
    inv = (fwd * w[:, None]).T
    return fwd.astype(np.float32), inv.astype(np.float32)


def _filter_consts(L):
    t = np.arange(L, dtype=np.float64) / L
    fr = np.arange(1, FILT_FREQS + 1, dtype=np.float64)
    ang = 2.0 * math.pi * fr[:, None] * t[None, :]
    z = np.zeros((FILT_EMB_PAD, L), np.float64)
    z[0] = t
    z[1:1 + FILT_FREQS] = np.cos(ang)
    z[1 + FILT_FREQS:1 + 2 * FILT_FREQS] = np.sin(ang)
    deltas = np.abs(np.linspace(MIN_DECAY, MAX_DECAY, HY_CH))
    decay = np.exp(-t[:, None] * deltas[None, :])
    return z.astype(np.float32), decay.astype(np.float32)


def _filter_kernel(z_ref, w1_ref, b1_ref, w2_ref, b2_ref, w3_ref, b3_ref, fr_ref, dec_ref,
                   a_ref, o_ref, h_s, *, L):
    @pl.when(pl.program_id(0) == 0)
    def _():
        fr = fr_ref[...]
        h = jnp.sin(fr * (jnp.dot(w1_ref[...], z_ref[...], precision=HIGHEST,
                                  preferred_element_type=F32) + b1_ref[...]))
        h_s[...] = jnp.sin(fr * (jnp.dot(w2_ref[...], h, precision=HIGHEST,
                                         preferred_element_type=F32) + b2_ref[...]))

    tdot = lambda a, b: lax.dot_general(a, b, (((0,), (0,)), ((), ())), preferred_element_type=F32)
    hid, w3 = h_s[...], w3_ref[...]
    hh, wh = hid.astype(BF16), w3.astype(BF16)
    hl, wl = (hid - hh.astype(F32)).astype(BF16), (w3 - wh.astype(F32)).astype(BF16)
    h = tdot(hh, wh) + tdot(hl, wh) + tdot(hh, wl) + b3_ref[...]
    dec = dec_ref[...]
    row0 = lax.broadcasted_iota(jnp.int32, (L, 1), 0) == 0
    fwd = h[:, :HY_CH] * dec
    bwd = jnp.where(row0, 0.0, h[:, HY_CH:] * dec)
    gp = (fwd + bwd).astype(BF16)
    gm = (fwd - bwd).astype(BF16)
    top = _dot(a_ref[0:L, :], gp)
    bot = _dot(a_ref[L:2 * L, :], gm)
    nyq = _dot(a_ref[L:L + 16, :], gp)[0:1]
    o_ref[0, 0:L, :] = top
    o_ref[0, L:2 * L, :] = jnp.where(row0, nyq, bot)


def _filter_spectrum(L, a_fwd, w1, b1, w2, b2, w3, b3, freq, casts=()):
    z, decay = _filter_consts(L)
    full = lambda shape: pl.BlockSpec(shape, lambda o: (0,) * len(shape))
    col = lambda v: v[:, None]
    cast_specs, cast_shapes = _cast_specs(casts, 2)
    return pl.pallas_call(
        _with_casts(functools.partial(_filter_kernel, L=L), 10, 1, len(casts)),
        grid=(2,),
        in_specs=[
            full((FILT_EMB_PAD, L)),
            full((FILT_HID, FILT_EMB_PAD)),
            full((FILT_HID, 1)),
            full((FILT_HID, FILT_HID)),
            full((FILT_HID, 1)),
            pl.BlockSpec((FILT_HID, 2 * HY_CH), lambda o: (0, o)),
            pl.BlockSpec((1, 2 * HY_CH), lambda o: (0, o)),
            full((FILT_HID, 1)),
            full((L, HY_CH)),
            full((2 * L, L)),
        ] + cast_specs,
        out_specs=[pl.BlockSpec((1, 2 * L, HY_CH), lambda o: (o, 0, 0))] + cast_specs,
        out_shape=[jax.ShapeDtypeStruct((2, 2 * L, HY_CH), F32)] + cast_shapes,
        scratch_shapes=[pltpu.VMEM((FILT_HID, L), F32)],
        compiler_params=_cparams(("arbitrary",)),
        name=f"filter_{L}",
    )(jnp.asarray(z), jnp.pad(w1, ((0, FILT_EMB_PAD - w1.shape[0]), (0, 0))).T, col(b1), w2.T, col(b2),
      w3, b3[None], col(freq), jnp.asarray(decay), a_fwd, *casts)


def _store_heads_major(x, dst_ref, L):
    xt = x.T
    for s in range(x.shape[0] // L):
        for h in range(N_HEADS):
            dst_ref[s, 0, h] = xt[h * HEAD_DIM:(h + 1) * HEAD_DIM, s * L:(s + 1) * L]


def _proj_kernel(x_ref, mods_ref, g_ref, w_ref, qkv_ref, hy_ref, *state, mod_base, blocks_per_mod, L):
    r = mod_base + pl.program_id(0) // blocks_per_mod
    sh1 = mods_ref[0, pl.ds(r, 1), :]
    sc1 = mods_ref[1, pl.ds(r, 1), :]
    h = (_rms(x_ref[...]) * g_ref[...] * (1.0 + sc1) + sh1).astype(BF16)
    q = _dot(h, w_ref[:, 0:ATTN_W])
    qkv_ref[:, 0:ATTN_W] = (q * HEAD_DIM ** -0.5).astype(BF16)
    k = _dot(h, w_ref[:, ATTN_W:2 * ATTN_W])
    qkv_ref[:, ATTN_W:2 * ATTN_W] = k.astype(BF16)
    v = _dot(h, w_ref[:, 2 * ATTN_W:3 * ATTN_W])
    qkv_ref[:, 2 * ATTN_W:3 * ATTN_W] = v.astype(BF16)
    if state:
        ks_ref, vs_ref = state
        _store_heads_major(k, ks_ref, L)
        _store_heads_major(v, vs_ref, L)
    hy_ref[...] = _dot(h, w_ref[:, 3 * ATTN_W:])


def _proj(x, mods, g, w_in, *, mod_base, blocks_per_mod, L, emit_state):
    n = x.shape[0]
    row = lambda w: pl.BlockSpec((PROJ_ROWS, w), lambda i: (i, 0))
    out_specs = [row(3 * ATTN_W), row(3 * HY_CH)]
    out_shape = [jax.ShapeDtypeStruct((n, 3 * ATTN_W), BF16), jax.ShapeDtypeStruct((n, 3 * HY_CH), F32)]
    if emit_state:
        seqs = PROJ_ROWS // L
        state = pl.BlockSpec((seqs, 1, N_HEADS, HEAD_DIM, L), lambda i: (i, 0, 0, 0, 0))
        out_specs += [state, state]
        out_shape += [jax.ShapeDtypeStruct((n // L, 1, N_HEADS, HEAD_DIM, L), F32)] * 2
    return pl.pallas_call(
        functools.partial(_proj_kernel, mod_base=mod_base, blocks_per_mod=blocks_per_mod, L=L),
        grid=(n // PROJ_ROWS,),
        in_specs=[
            row(D_MODEL),
            pl.BlockSpec((6, 8, D_MODEL), lambda i: (0, 0, 0)),
            pl.BlockSpec((1, D_MODEL), lambda i: (0, 0)),
            pl.BlockSpec((D_MODEL, IN_COLS), lambda i: (0, 0)),
        ],
        out_specs=out_specs,
        out_shape=out_shape,
        compiler_params=_cparams(("arbitrary",)),
        name="proj_ctx" if emit_state else "proj_lat",
    )(x, mods, g, w_in)


def _softmax_pv(scores, values):
    mx = functools.reduce(jnp.maximum, [jnp.max(s, axis=-1, keepdims=True) for s in scores])
    return functools.reduce(jnp.add, [_dot(jnp.exp(s - mx).astype(BF16), v) for s, v in zip(scores, values)])


def _normalise_heads(o0, o1, low):
    den = pltpu.roll(jnp.where(low, o1, o0), HEAD_DIM, 1)
    return jnp.where(low, o0, o1) * (1.0 / den)


def _split_heads(k, v, masks):
    ks = [jnp.where(m, k, jnp.zeros_like(k)) for m in masks]
    vs = [jnp.where(m, v, jnp.ones_like(v)) for m in masks]
    return ks, vs


def _ctx_attn_kernel(qkv_ref, o_ref, *, n_seq, L):
    low = lax.broadcasted_iota(jnp.int32, (1, LANES), 1) < HEAD_DIM
    masks = (low, jnp.logical_not(low))

    def seq_body(s, carry):
        rows = pl.ds(pl.multiple_of(s * L, L), L)
        for p in range(ATTN_W // LANES):
            cols = lambda part: slice(part * ATTN_W + p * LANES, part * ATTN_W + (p + 1) * LANES)
            qb = qkv_ref[rows, cols(0)]
            kb = qkv_ref[rows, cols(1)]
            vb = qkv_ref[rows, cols(2)]
            outs = []
            for m in masks:
                s_ = _dot_t(qb, jnp.where(m, kb, jnp.zeros_like(kb)))
                p_ = jnp.exp(s_ - jnp.max(s_, axis=-1, keepdims=True))
                outs.append(_dot(p_.astype(BF16), vb) * (1.0 / jnp.sum(p_, axis=-1, keepdims=True)))
            o_ref[rows, p * LANES:(p + 1) * LANES] = jnp.where(low, outs[0], outs[1])
        return carry

    lax.fori_loop(0, n_seq, seq_body, 0)


def _ctx_attn(qkv, L, casts=()):
    n = qkv.shape[0]
    cast_specs, cast_shapes = _cast_specs(casts, n // ROW_BLOCK)
    return pl.pallas_call(
        _with_casts(functools.partial(_ctx_attn_kernel, n_seq=ROW_BLOCK // L, L=L), 1, 1, len(casts)),
        grid=(n // ROW_BLOCK,),
        in_specs=[pl.BlockSpec((ROW_BLOCK, 3 * ATTN_W), lambda i: (i, 0))] + cast_specs,
        out_specs=[pl.BlockSpec((ROW_BLOCK, ATTN_W), lambda i: (i, 0))] + cast_specs,
        out_shape=[jax.ShapeDtypeStruct((n, ATTN_W), F32)] + cast_shapes,
        compiler_params=_cparams(("arbitrary",)),
        name="ctx_attn",
    )(qkv, *casts)


N_DR = 2 * WIN_ROWS - 1
N_DC = 2 * WIN_COLS - 1


def _rpb_expand_table():
    d = np.clip(np.arange(LANES) - GRID_W, -(WIN_COLS - 1), WIN_COLS - 1) + (WIN_COLS - 1)
    t = np.zeros((32, LANES), np.float32)
    t[d, np.arange(LANES)] = 1.0
    return t


def _bias_tile_index(n_rows):
    n_blk = n_rows // Q_ROWS
    idx = np.full((n_blk, Q_ROWS, KEY_ROWS), N_DR, np.int32)
    for i in range(n_blk):
        ks = min(max(Q_ROWS * i - WIN_ROWS // 2, 0), n_rows - KEY_ROWS)
        for a in range(Q_ROWS):
            r = Q_ROWS * i + a
            rs = min(max(r - WIN_ROWS // 2, 0), n_rows - WIN_ROWS)
            for j in range(KEY_ROWS):
                if rs <= ks + j < rs + WIN_ROWS:
                    idx[i, a, j] = ks + j - r + (WIN_ROWS - 1)
            assert (idx[i, a] != N_DR).sum() == WIN_ROWS, "key window must cover the whole band"
    return idx


def _build_bias(rpb_ref, ext_ref, tw_ref, bias_ref, n_blk):
    shape = (GRID_W, LANES)
    lane = lax.broadcasted_iota(jnp.int32, shape, 1)
    qc = lax.broadcasted_iota(jnp.int32, shape, 0)
    kc = lane & (GRID_W - 1)
    low = lane < GRID_W
    col_start = jnp.clip(qc - WIN_COLS // 2, 0, GRID_W - WIN_COLS)
    col_in = (kc >= col_start) & (kc < col_start + WIN_COLS)
    idx = _bias_tile_index(Q_ROWS * n_blk)
    for e in range(2):
        g = jnp.dot(rpb_ref[e], ext_ref[...], precision=HIGHEST, preferred_element_type=F32)
        for dr in range(N_DR):
            t = pltpu.roll(jnp.broadcast_to(g[dr:dr + 1, :], shape), GRID_W, 1, stride=1, stride_axis=0)
            t = jnp.where(low, t, pltpu.roll(t, GRID_W, 1))
            tw_ref[e, dr] = jnp.where(col_in, t, NEG)
        tw_ref[e, N_DR] = jnp.full(shape, NEG, F32)
        for i in range(n_blk):
            for a in range(Q_ROWS):
                for jt in range(KEY_ROWS // 2):
                    tile = jnp.where(low, tw_ref[e, int(idx[i, a, 2 * jt])], tw_ref[e, int(idx[i, a, 2 * jt + 1])])
                    bias_ref[e, i, a * GRID_W:(a + 1) * GRID_W, jt * LANES:(jt + 1) * LANES] = tile


def _lat_attn_kernel(q_ref, k_ref, v_ref, kc_ref, vc_ref, rpb_ref, ext_ref, o_ref, km_ref, vm_ref, tw_ref,
                     bias_ref, *, n_blk):
    @pl.when(pl.program_id(1) == 0)
    def _():
        _build_bias(rpb_ref, ext_ref, tw_ref, bias_ref, n_blk)

    low = lax.broadcasted_iota(jnp.int32, (1, LANES), 1) < HEAD_DIM
    masks = (low, jnp.logical_not(low))
    ks, vs = _split_heads(k_ref[...], v_ref[...], masks)
    for e in range(2):
        km_ref[e] = ks[e]
        vm_ref[e] = vs[e]
    kce, vce = _split_heads(kc_ref[0], vc_ref[0], masks)
    nq = Q_ROWS * GRID_W
    nk = KEY_ROWS * GRID_W
    for i in range(n_blk):
        qrows = slice(i * nq, (i + 1) * nq)
        k0 = min(max(Q_ROWS * i - WIN_ROWS // 2, 0), Q_ROWS * n_blk - KEY_ROWS) * GRID_W
        krows = slice(k0, k0 + nk)
        qb = q_ref[qrows, :]
        outs = []
        for e in range(2):
            s_loc = _dot_t(qb, km_ref[e, krows, :]) + bias_ref[e, i]
            s_ctx = _dot_t(qb, kce[e])
            outs.append(_softmax_pv([s_loc, s_ctx], [vm_ref[e, krows, :], vce[e]]))
        o_ref[qrows, :] = _normalise_heads(outs[0], outs[1], low)


def _lat_attn(qkv, kc, vc, rpb, batch, L):
    n_blk = L // (Q_ROWS * GRID_W)
    n_pairs = ATTN_W // LANES
    col = lambda part: pl.BlockSpec((L, LANES), lambda p, b: (b, part * n_pairs + p))
    cache = pl.BlockSpec((1, kc.shape[1], LANES), lambda p, b: (b, 0, p))
    rpb_pad = jnp.pad(rpb, ((0, 0), (0, 16 - N_DR), (0, 32 - N_DC)))
    return pl.pallas_call(
        functools.partial(_lat_attn_kernel, n_blk=n_blk),
        grid=(n_pairs, batch),
        in_specs=[col(0), col(1), col(2), cache, cache,
                  pl.BlockSpec((2, 16, 32), lambda p, b: (p, 0, 0)),
                  pl.BlockSpec((32, LANES), lambda p, b: (0, 0))],
        out_specs=pl.BlockSpec((L, LANES), lambda p, b: (b, p)),
        out_shape=jax.ShapeDtypeStruct((batch * L, ATTN_W), F32),
        scratch_shapes=[pltpu.VMEM((2, L, LANES), BF16), pltpu.VMEM((2, L, LANES), BF16),
                        pltpu.VMEM((2, 16, GRID_W, LANES), F32),
                        pltpu.VMEM((2, n_blk, Q_ROWS * GRID_W, KEY_ROWS * GRID_W), F32)],
        compiler_params=_cparams(("arbitrary", "arbitrary")),
        name="lat_attn",
    )(qkv, qkv, qkv, kc, vc, rpb_pad, jnp.asarray(_rpb_expand_table()))


SUBLANES = 8


def _row_neighbours(x, L):
    n, c = x.shape
    t, tl = n // SUBLANES, L // SUBLANES
    x3 = x.reshape(t, SUBLANES, c)
    sub = lax.broadcasted_iota(jnp.int32, (1, SUBLANES, 1), 1)
    zero = jnp.zeros((1, SUBLANES, c), x.dtype)
    down = pltpu.roll(x3, 1, 1)
    up = pltpu.roll(x3, SUBLANES - 1, 1)
    down_prev = jnp.concatenate([p for s in range(0, t, tl) for p in (zero, down[s:s + tl - 1])], axis=0)
    up_next = jnp.concatenate([p for s in range(0, t, tl) for p in (up[s + 1:s + tl], zero)], axis=0)
    above = jnp.where(sub == 0, down_prev, down).reshape(n, c)
    below = jnp.where(sub == SUBLANES - 1, up_next, up).reshape(n, c)
    return above, below


def _dwconv3(x, w_ref, b_ref, L):
    above, below = _row_neighbours(x, L)
    return above * w_ref[0:1, :] + x * w_ref[1:2, :] + below * w_ref[2:3, :] + b_ref[...]


def _hyena_kernel(h1_ref, h2_ref, hv_ref, w1_ref, w2_ref, wv_ref, b1_ref, b2_ref, bv_ref,
                  kf_ref, fb_ref, a_ref, ai_ref, o_ref, *, L):
    row0 = lax.broadcasted_iota(jnp.int32, (L, 1), 0) == 0

    def long_conv(u, order):
        ub = u.astype(BF16)
        ur = _dot(a_ref[0:L, :], ub)
        ui = _dot(a_ref[L:2 * L, :], ub)
        kr = kf_ref[order, 0:L, :]
        ki = kf_ref[order, L:2 * L, :]
        yr = jnp.where(row0, ur * kr, ur * kr - ui * ki)
        yi = jnp.where(row0, ui * ki, ur * ki + ui * kr)
        y = _dot(ai_ref[:, 0:L], yr.astype(BF16)) + _dot(ai_ref[:, L:2 * L], yi.astype(BF16))
        return y + u * fb_ref[order:order + 1, :]

    def one_sequence(rows):
        v = _dwconv3(hv_ref[rows, :], wv_ref, bv_ref, L)
        z = _dwconv3(h1_ref[rows, :], w1_ref, b1_ref, L) * long_conv(v, 0)
        o_ref[rows, :] = _dwconv3(h2_ref[rows, :], w2_ref, b2_ref, L) * long_conv(z, 1)

    n_seq = h1_ref.shape[0] // L
    if n_seq == 1:
        one_sequence(slice(0, L))
    else:
        def seq_body(s, carry):
            one_sequence(pl.ds(pl.multiple_of(s * L, L), L))
            return carry

        lax.fori_loop(0, n_seq, seq_body, 0)


def _hyena(hy, conv_w, conv_b, kf, filt_bias, a_fwd, a_inv, L, casts=()):
    n = hy.shape[0]
    cast_specs, cast_shapes = _cast_specs(casts, n // ROW_BLOCK)
    part = lambda k, rows: pl.BlockSpec((rows, HY_CH), lambda i: (i if rows == ROW_BLOCK else 0, k))
    once = lambda shape: pl.BlockSpec(shape, lambda i: (0,) * len(shape), pipeline_mode=pl.Buffered(1))
    return pl.pallas_call(
        _with_casts(functools.partial(_hyena_kernel, L=L), 13, 1, len(casts)),
        grid=(n // ROW_BLOCK,),
        in_specs=[part(0, ROW_BLOCK), part(1, ROW_BLOCK), part(2, ROW_BLOCK),
                  part(0, 3), part(1, 3), part(2, 3),
                  part(0, 1), part(1, 1), part(2, 1),
                  once((2, 2 * L, HY_CH)), once((2, HY_CH)), once((2 * L, L)), once((L, 2 * L))] + cast_specs,
        out_specs=[pl.BlockSpec((ROW_BLOCK, HY_CH), lambda i: (i, 0))] + cast_specs,
        out_shape=[jax.ShapeDtypeStruct((n, HY_CH), F32)] + cast_shapes,
        compiler_params=_cparams(("arbitrary",)),
        name=f"hyena_{L}",
    )(hy, hy, hy, conv_w, conv_w, conv_w, conv_b, conv_b, conv_b, kf, filt_bias, a_fwd, a_inv, *casts)


def _tail_kernel(*refs, L, mod_base, blocks_per_mod, halo, n_blocks):
    rows_refs, rest = refs[:9 if halo else 3], refs[9 if halo else 3:]
    (mods_ref, gg_ref, wo_ref, n2_ref, fg_ref, wu_ref, cw_ref, cb_ref, wd_ref, o_ref,
     h2_s, x1_s, acc_s, ma_s, mb_s, mc_s) = rest
    rows = o_ref.shape[0]
    i = pl.program_id(0)
    ext = rows + 2 * halo
    base = TAIL_PAD + halo
    n_tiles, seq_tiles = rows // SUBLANES, L // SUBLANES
    last_tile = N_FF_TILES - 1
    sub = lax.broadcasted_iota(jnp.int32, (1, SUBLANES, 1), 1)
    zero_tile = jnp.zeros((1, SUBLANES, FF_TILE), F32)

    def mod(k, blk):
        return mods_ref[k, pl.ds(mod_base + blk // blocks_per_mod, 1), :]

    def up_proj(j, dst):
        for e in range(2):
            cols = pl.ds(pl.multiple_of(e * D_FF + j * FF_TILE, FF_TILE), FF_TILE)
            dst[e, TAIL_PAD:TAIL_PAD + ext, :] = _dot(h2_s[...], wu_ref[:, cols])

    def conv(src, e, j, blk):
        t = src[e, base - SUBLANES:base + rows + SUBLANES, :].reshape(n_tiles + 2, SUBLANES, FF_TILE)
        down = pltpu.roll(t, 1, 1)
        up = pltpu.roll(t, SUBLANES - 1, 1)
        if L >= rows:
            starts_seq = (blk * rows) & (L - 1) == 0
            ends_seq = ((blk + 1) * rows) & (L - 1) == 0
            above = [jnp.where(starts_seq, zero_tile, down[0:1]), down[1:n_tiles]]
            below = [up[2:n_tiles + 1], jnp.where(ends_seq, zero_tile, up[n_tiles + 1:n_tiles + 2])]
        else:
            above, below = [], []
            for s in range(0, n_tiles, seq_tiles):
                above += [down[0:1] if s == 0 else zero_tile, down[s + 1:s + seq_tiles]]
                last_seq = s + seq_tiles == n_tiles
                below += [up[s + 2:s + seq_tiles + 1], up[n_tiles + 1:n_tiles + 2] if last_seq else zero_tile]
        above = jnp.where(sub == 0, jnp.concatenate(above, axis=0), down[1:n_tiles + 1])
        below = jnp.where(sub == SUBLANES - 1, jnp.concatenate(below, axis=0), up[1:n_tiles + 1])
        w = cw_ref[e, j]
        return above * w[0:1] + t[1:n_tiles + 1] * w[1:2] + below * w[2:3] + cb_ref[e, j]

    def down_proj(j, src, blk):
        gate = conv(src, 0, j, blk)
        act = gate / (1.0 + jnp.exp(-gate)) * conv(src, 1, j, blk)
        return _dot(act.reshape(rows, FF_TILE).astype(BF16), wd_ref[j])

    def start_block(blk):
        if halo:
            x_ref, a_ref, hy_ref, xp_ref, ap_ref, hp_ref, xn_ref, an_ref, hn_ref = rows_refs
            cat = lambda p, m, n: jnp.concatenate([p[...], m[...], n[...]], axis=0)
            x, a, hy = cat(xp_ref, x_ref, xn_ref), cat(ap_ref, a_ref, an_ref), cat(hp_ref, hy_ref, hn_ref)
        else:
            x, a, hy = (ref[...] for ref in rows_refs)
        gg = gg_ref[...]
        merged = jnp.concatenate([_rms(a) * gg[:, :ATTN_W], _rms(hy) * gg[:, ATTN_W:]], axis=-1).astype(BF16)
        x1 = x + mod(2, blk) * _dot(merged, wo_ref[...])
        h2_s[...] = (_rms(x1) * n2_ref[...] * (1.0 + mod(4, blk)) + mod(3, blk)).astype(BF16)
        x1_s[...] = x1[halo:halo + rows]
        up_proj(0, mc_s)
        up_proj(1, mb_s)
        acc_s[...] = down_proj(0, mc_s, blk)
        up_proj(2, ma_s)
        acc_s[...] += down_proj(1, mb_s, blk)

    def finish_block(blk):
        x2 = x1_s[...] + mod(5, blk) * (acc_s[...] + down_proj(last_tile, ma_s, blk))
        o_ref[...] = _rms(x2) * fg_ref[...]

    @pl.when(i == 0)
    def _():
        for buf in (ma_s, mb_s, mc_s):
            for e in range(2):
                buf[e, 0:TAIL_PAD, :] = jnp.zeros((TAIL_PAD, FF_TILE), F32)
                buf[e, TAIL_PAD + ext:, :] = jnp.zeros((TAIL_PAD, FF_TILE), F32)
        start_block(i)

    @pl.when(jnp.logical_and(i > 0, i < n_blocks))
    def _():
        finish_block(i - 1)
        start_block(i)

    @pl.when(i == n_blocks)
    def _():
        finish_block(i - 1)

    @pl.when(i < n_blocks)
    def _():
        def two_tiles(k, carry):
            j = 2 * k
            up_proj(j + 1, mb_s)
            acc_s[...] += down_proj(j, ma_s, i)
            up_proj(j + 2, ma_s)
            acc_s[...] += down_proj(j + 1, mb_s, i)
            return carry

        lax.fori_loop(1, (N_FF_TILES - 1) // 2, two_tiles, 0)


def _tail(x, a, hyo, mods, grp_g, w_out, n2_g, final_g, w_up, conv_w, conv_b, w_down, *, L, mod_base,
          blocks_per_mod):
    n = x.shape[0]
    n_blocks = n // TAIL_ROWS
    halo = TAIL_HALO if L > TAIL_ROWS else 0
    cur = lambda i: jnp.minimum(i, n_blocks - 1)
    row = lambda w: pl.BlockSpec((TAIL_ROWS, w), lambda i: (cur(i), 0))
    per_halo = TAIL_ROWS // TAIL_HALO
    prev = lambda w: pl.BlockSpec((TAIL_HALO, w), lambda i: (jnp.maximum(cur(i) * per_halo - 1, 0), 0))
    nxt = lambda w: pl.BlockSpec(
        (TAIL_HALO, w), lambda i: (jnp.minimum((cur(i) + 1) * per_halo, n // TAIL_HALO - 1), 0))
    once = lambda shape: pl.BlockSpec(shape, lambda i: (0,) * len(shape), pipeline_mode=pl.Buffered(1))
    widths = (D_MODEL, ATTN_W, HY_CH)
    row_specs = [row(w) for w in widths]
    row_args = [x, a, hyo]
    if halo:
        row_specs += [prev(w) for w in widths] + [nxt(w) for w in widths]
        row_args += [x, a, hyo, x, a, hyo]
    ext = TAIL_ROWS + 2 * halo
    staging = pltpu.VMEM((2, ext + 2 * TAIL_PAD, FF_TILE), F32)
    return pl.pallas_call(
        functools.partial(_tail_kernel, L=L, mod_base=mod_base, blocks_per_mod=blocks_per_mod, halo=halo,
                          n_blocks=n_blocks),
        grid=(n_blocks + 1,),
        in_specs=row_specs + [
            once((6, 8, D_MODEL)), once((1, D_MODEL)), once((D_MODEL, D_MODEL)), once((1, D_MODEL)),
            once((1, D_MODEL)),
            once((D_MODEL, 2 * D_FF)),
            once((2, N_FF_TILES, 3, FF_TILE)), once((2, N_FF_TILES, 1, FF_TILE)),
            once((N_FF_TILES, FF_TILE, D_MODEL))],
        out_specs=pl.BlockSpec((TAIL_ROWS, D_MODEL), lambda i: (jnp.maximum(i - 1, 0), 0)),
        out_shape=jax.ShapeDtypeStruct((n, D_MODEL), F32),
        scratch_shapes=[pltpu.VMEM((ext, D_MODEL), BF16), pltpu.VMEM((TAIL_ROWS, D_MODEL), F32),
                        pltpu.VMEM((TAIL_ROWS, D_MODEL), F32), staging, staging, staging],
        compiler_params=_cparams(("arbitrary",)),
        name=f"tail_{L}",
    )(*row_args, mods, grp_g, w_out, n2_g, final_g, w_up, conv_w, conv_b, w_down)


def kernel(x_prompt, x_sample, cache_ctx_k, cache_ctx_v, c, c_ctx, w_ada, b_ada, norm1_g, w_in, rpb,
           hy_conv_w, hy_conv_b, filt_w1, filt_b1, filt_w2, filt_b2, filt_w3, filt_b3, filt_freq,
           filt_bias, grp_norm_g, w_out, norm2_g, w_up, ffn_conv_w, ffn_conv_b, w_down, final_g):
    assert w_ada.shape[0] == 1, "single-layer trunk"
    bc, lc, _ = x_prompt.shape
    bl, ll, _ = x_sample.shape
    past = cache_ctx_k.shape[3]

    cvec = jnp.concatenate([c_ctx[None], c, jnp.zeros((8 - 1 - bl, D_MODEL), F32)], axis=0)
    mods = _mods(cvec, w_ada[0], b_ada[0])

    conv_w_t = ffn_conv_w[0].reshape(3, 2, N_FF_TILES, FF_TILE).transpose(1, 2, 0, 3)
    conv_b_t = ffn_conv_b[0].reshape(2, N_FF_TILES, 1, FF_TILE)
    g1 = norm1_g[0][None]
    filt = (filt_w1[0], filt_b1[0], filt_w2[0], filt_b2[0], filt_w3[0], filt_b3[0], filt_freq[0])

    def tables(L):
        fwd, inv = _dft_tables(L)
        return jnp.asarray(fwd).astype(BF16), jnp.asarray(inv).astype(BF16)

    xc = x_prompt.reshape(bc * lc, D_MODEL)
    a_fwd, a_inv = tables(lc)
    kf, w_in_b = _filter_spectrum(lc, a_fwd, *filt, casts=(w_in[0],))
    qkv, hy, state_k, state_v = _proj(xc, mods, g1, w_in_b, mod_base=0, blocks_per_mod=bc * lc // PROJ_ROWS,
                                      L=lc, emit_state=True)
    att, w_up_b = _ctx_attn(qkv, lc, casts=(w_up[0],))
    att, xs = lax.optimization_barrier((att, x_sample.reshape(bl * ll, D_MODEL)))
    hyo, w_down_b, w_out_b = _hyena(hy, hy_conv_w[0], hy_conv_b[0][None], kf, filt_bias[0], a_fwd, a_inv, lc,
                                    casts=(w_down[0], w_out[0]))
    shared_tail = (grp_norm_g[0][None], w_out_b, norm2_g[0][None], final_g[None], w_up_b, conv_w_t, conv_b_t,
                   w_down_b.reshape(N_FF_TILES, FF_TILE, D_MODEL))
    y_prompt = _tail(xc, att, hyo, mods, *shared_tail, L=lc, mod_base=0,
                     blocks_per_mod=bc * lc // TAIL_ROWS).reshape(bc, lc, D_MODEL)

    a_fwd, a_inv = tables(ll)
    kf, = _filter_spectrum(ll, a_fwd, *filt)
    qkv, hy = _proj(xs, mods, g1, w_in_b, mod_base=1, blocks_per_mod=ll // PROJ_ROWS, L=ll, emit_state=False)
    lanes_major = lambda t: t[:, 0].transpose(0, 2, 1, 3).reshape(bl, past, ATTN_W).astype(BF16)
    hyo, = _hyena(hy, hy_conv_w[0], hy_conv_b[0][None], kf, filt_bias[0], a_fwd, a_inv, ll)
    hyo, qkv = lax.optimization_barrier((hyo, qkv))
    att = _lat_attn(qkv, lanes_major(cache_ctx_k), lanes_major(cache_ctx_v), rpb[0], bl, ll)
    y_sample = _tail(xs, att, hyo, mods, *shared_tail, L=ll, mod_base=1,
                     blocks_per_mod=ll // TAIL_ROWS).reshape(bl, ll, D_MODEL)

    return (y_prompt, y_sample, jnp.swapaxes(state_k, 3, 4), jnp.swapaxes(state_v, 3, 4))
```

```python
import functools
import math

import numpy as np
import jax
import jax.numpy as jnp
from jax import lax
from jax.experimental import pallas as pl
from jax.experimental.pallas import tpu as pltpu

F32 = jnp.float32
BF16 = jnp.bfloat16
HIGHEST = lax.Precision.HIGHEST

D_MODEL = 1024
N_HEADS = 8
HEAD_DIM = 64
ATTN_W = N_HEADS * HEAD_DIM
HY_CH = D_MODEL - ATTN_W
IN_COLS = 3 * ATTN_W + 3 * HY_CH
D_FF = 2816
GRID_W = 64
WIN_ROWS = 8
WIN_COLS = 16
FILT_FREQS = 8
FILT_HID = 64
FILT_EMB_PAD = 32
DECAY_TARGET = 1e-2
MAX_DECAY = math.log(DECAY_TARGET) / 0.3
MIN_DECAY = math.log(DECAY_TARGET) / 1.5
EPS = 1e-6
NEG = -1e30

LANES = 128
ROW_BLOCK = 1024
PROJ_ROWS = 512
TAIL_ROWS = 512
TAIL_HALO = 16
TAIL_PAD = 8
FF_TILE = 256
N_FF_TILES = D_FF // FF_TILE
assert N_FF_TILES % 2 == 1 and N_FF_TILES >= 3, "the ConvFFN pipeline peels an odd tile count"
Q_ROWS = 4
KEY_ROWS = 12
VMEM_LIMIT = 56 * 1024 * 1024


def _cparams(sem):
    return pltpu.CompilerParams(dimension_semantics=sem, vmem_limit_bytes=VMEM_LIMIT)


def _rms(x):
    return x * lax.rsqrt(jnp.mean(x * x, axis=-1, keepdims=True) + EPS)


def _dot(a, b):
    return jnp.dot(a, b, preferred_element_type=F32)


def _dot3(a, b):
    ah = a.astype(BF16)
    bh = b.astype(BF16)
    al = (a - ah.astype(F32)).astype(BF16)
    bl = (b - bh.astype(F32)).astype(BF16)
    return _dot(ah, bh) + _dot(al, bh) + _dot(ah, bl)


def _dot_t(a, b):
    return lax.dot_general(a, b, (((1,), (1,)), ((), ())), preferred_element_type=F32)


def _with_casts(body, n_in, n_out, n_cast):
    def kernel(*refs):
        a, b, c = n_in + n_cast, n_in + n_cast + n_out, n_in + 2 * n_cast + n_out
        for src, dst in zip(refs[n_in:a], refs[b:c]):
            dst[...] = src[...].astype(BF16)
        body(*refs[:n_in], *refs[a:b], *refs[c:])

    return kernel


def _cast_specs(casts, n_steps):
    specs = [pl.BlockSpec((w.shape[0] // n_steps, w.shape[1]), lambda i: (i, 0)) for w in casts]
    return specs, [jax.ShapeDtypeStruct(w.shape, BF16) for w in casts]


def _mods_kernel(c_ref, w_ref, b_ref, o_ref):
    cv = c_ref[...]
    s = cv / (1.0 + jnp.exp(-cv))
    o_ref[0] = _dot3(s, w_ref[...]) + b_ref[0]


def _mods(cvec, w_ada, b_ada):
    return pl.pallas_call(
        _mods_kernel,
        grid=(6,),
        in_specs=[
            pl.BlockSpec((8, D_MODEL), lambda j: (0, 0)),
            pl.BlockSpec((D_MODEL, D_MODEL), lambda j: (0, j)),
            pl.BlockSpec((1, 1, D_MODEL), lambda j: (j, 0, 0)),
        ],
        out_specs=pl.BlockSpec((1, 8, D_MODEL), lambda j: (j, 0, 0)),
        out_shape=jax.ShapeDtypeStruct((6, 8, D_MODEL), F32),
        compiler_params=_cparams(("arbitrary",)),
        name="mods",
    )(cvec, w_ada, b_ada.reshape(6, 1, D_MODEL))


def _dft_tables(L):
    f = np.arange(L, dtype=np.int64)[:, None]
    t = np.arange(L, dtype=np.int64)[None, :]
    ang = np.pi * ((f * t) % (2 * L)).astype(np.float64) / L
    top = np.cos(ang)
    bot = -np.sin(ang)
    bot[0, :] = np.where(np.arange(L) % 2 == 0, 1.0, -1.0)
    fwd = np.concatenate([top, bot], axis=0)
    w = np.full((2 * L,), 1.0 / L)
    w[0] = w[L] = 0.5 / L
    inv = (fwd * w[:, None]).T
    return fwd.astype(np.float32), inv.astype(np.float32)


def _filter_consts(L):
    t = np.arange(L, dtype=np.float64) / L
    fr = np.arange(1, FILT_FREQS + 1, dtype=np.float64)
    ang = 2.0 * math.pi * fr[:, None] * t[None, :]
    z = np.zeros((FILT_EMB_PAD, L), np.float64)
    z[0] = t
    z[1:1 + FILT_FREQS] = np.cos(ang)
    z[1 + FILT_FREQS:1 + 2 * FILT_FREQS] = np.sin(ang)
    deltas = np.abs(np.linspace(MIN_DECAY, MAX_DECAY, HY_CH))
    decay = np.exp(-t[:, None] * deltas[None, :])
    return z.astype(np.float32), decay.astype(np.float32)


def _filter_kernel(z_ref, w1_ref, b1_ref, w2_ref, b2_ref, w3_ref, b3_ref, fr_ref, dec_ref,
                   a_ref, o_ref, h_s, *, L):
    @pl.when(pl.program_id(0) == 0)
    def _():
        fr = fr_ref[...]
        h = jnp.sin(fr * (jnp.dot(w1_ref[...], z_ref[...], precision=HIGHEST,
                                  preferred_element_type=F32) + b1_ref[...]))
        h_s[...] = jnp.sin(fr * (jnp.dot(w2_ref[...], h, precision=HIGHEST,
                                         preferred_element_type=F32) + b2_ref[...]))

    tdot = lambda a, b: lax.dot_general(a, b, (((0,), (0,)), ((), ())), preferred_element_type=F32)
    hid, w3 = h_s[...], w3_ref[...]
    hh, wh = hid.astype(BF16), w3.astype(BF16)
    hl, wl = (hid - hh.astype(F32)).astype(BF16), (w3 - wh.astype(F32)).astype(BF16)
    h = tdot(hh, wh) + tdot(hl, wh) + tdot(hh, wl) + b3_ref[...]
    dec = dec_ref[...]
    row0 = lax.broadcasted_iota(jnp.int32, (L, 1), 0) == 0
    fwd = h[:, :HY_CH] * dec
    bwd = jnp.where(row0, 0.0, h[:, HY_CH:] * dec)
    gp = (fwd + bwd).astype(BF16)
    gm = (fwd - bwd).astype(BF16)
    top = _dot(a_ref[0:L, :], gp)
    bot = _dot(a_ref[L:2 * L, :], gm)
    nyq = _dot(a_ref[L:L + 16, :], gp)[0:1]
    o_ref[0, 0:L, :] = top
    o_ref[0, L:2 * L, :] = jnp.where(row0, nyq, bot)


def _filter_spectrum(L, a_fwd, w1, b1, w2, b2, w3, b3, freq, casts=()):
    z, decay = _filter_consts(L)
    full = lambda shape: pl.BlockSpec(shape, lambda o: (0,) * len(shape))
    col = lambda v: v[:, None]
    cast_specs, cast_shapes = _cast_specs(casts, 2)
    return pl.pallas_call(
        _with_casts(functools.partial(_filter_kernel, L=L), 10, 1, len(casts)),
        grid=(2,),
        in_specs=[
            full((FILT_EMB_PAD, L)),
            full((FILT_HID, FILT_EMB_PAD)),
            full((FILT_HID, 1)),
            full((FILT_HID, FILT_HID)),
            full((FILT_HID, 1)),
            pl.BlockSpec((FILT_HID, 2 * HY_CH), lambda o: (0, o)),
            pl.BlockSpec((1, 2 * HY_CH), lambda o: (0, o)),
            full((FILT_HID, 1)),
            full((L, HY_CH)),
            full((2 * L, L)),
        ] + cast_specs,
        out_specs=[pl.BlockSpec((1, 2 * L, HY_CH), lambda o: (o, 0, 0))] + cast_specs,
        out_shape=[jax.ShapeDtypeStruct((2, 2 * L, HY_CH), F32)] + cast_shapes,
        scratch_shapes=[pltpu.VMEM((FILT_HID, L), F32)],
        compiler_params=_cparams(("arbitrary",)),
        name=f"filter_{L}",
    )(jnp.asarray(z), jnp.pad(w1, ((0, FILT_EMB_PAD - w1.shape[0]), (0, 0))).T, col(b1), w2.T, col(b2),
      w3, b3[None], col(freq), jnp.asarray(decay), a_fwd, *casts)


def _store_heads_major(x, dst_ref, L):
    xt = x.T
    for s in range(x.shape[0] // L):
        for h in range(N_HEADS):
            dst_ref[s, 0, h] = xt[h * HEAD_DIM:(h + 1) * HEAD_DIM, s * L:(s + 1) * L]


def _proj_kernel(x_ref, mods_ref, g_ref, w_ref, qkv_ref, hy_ref, *state, mod_base, blocks_per_mod, L):
    r = mod_base + pl.program_id(0) // blocks_per_mod
    sh1 = mods_ref[0, pl.ds(r, 1), :]
    sc1 = mods_ref[1, pl.ds(r, 1), :]
    h = (_rms(x_ref[...]) * g_ref[...] * (1.0 + sc1) + sh1).astype(BF16)
    q = _dot(h, w_ref[:, 0:ATTN_W])
    qkv_ref[:, 0:ATTN_W] = (q * HEAD_DIM ** -0.5).astype(BF16)
    k = _dot(h, w_ref[:, ATTN_W:2 * ATTN_W])
    qkv_ref[:, ATTN_W:2 * ATTN_W] = k.astype(BF16)
    v = _dot(h, w_ref[:, 2 * ATTN_W:3 * ATTN_W])
    qkv_ref[:, 2 * ATTN_W:3 * ATTN_W] = v.astype(BF16)
    if state:
        ks_ref, vs_ref = state
        _store_heads_major(k, ks_ref, L)
        _store_heads_major(v, vs_ref, L)
    hy_ref[...] = _dot(h, w_ref[:, 3 * ATTN_W:])


def _proj(x, mods, g, w_in, *, mod_base, blocks_per_mod, L, emit_state):
    n = x.shape[0]
    row = lambda w: pl.BlockSpec((PROJ_ROWS, w), lambda i: (i, 0))
    out_specs = [row(3 * ATTN_W), row(3 * HY_CH)]
    out_shape = [jax.ShapeDtypeStruct((n, 3 * ATTN_W), BF16), jax.ShapeDtypeStruct((n, 3 * HY_CH), F32)]
    if emit_state:
        seqs = PROJ_ROWS // L
        state = pl.BlockSpec((seqs, 1, N_HEADS, HEAD_DIM, L), lambda i: (i, 0, 0, 0, 0))
        out_specs += [state, state]
        out_shape += [jax.ShapeDtypeStruct((n // L, 1, N_HEADS, HEAD_DIM, L), F32)] * 2
    return pl.pallas_call(
        functools.partial(_proj_kernel, mod_base=mod_base, blocks_per_mod=blocks_per_mod, L=L),
        grid=(n // PROJ_ROWS,),
        in_specs=[
            row(D_MODEL),
            pl.BlockSpec((6, 8, D_MODEL), lambda i: (0, 0, 0)),
            pl.BlockSpec((1, D_MODEL), lambda i: (0, 0)),
            pl.BlockSpec((D_MODEL, IN_COLS), lambda i: (0, 0)),
        ],
        out_specs=out_specs,
        out_shape=out_shape,
        compiler_params=_cparams(("arbitrary",)),
        name="proj_ctx" if emit_state else "proj_lat",
    )(x, mods, g, w_in)


def _softmax_pv(scores, values):
    mx = functools.reduce(jnp.maximum, [jnp.max(s, axis=-1, keepdims=True) for s in scores])
    return functools.reduce(jnp.add, [_dot(jnp.exp(s - mx).astype(BF16), v) for s, v in zip(scores, values)])


def _normalise_heads(o0, o1, low):
    den = pltpu.roll(jnp.where(low, o1, o0), HEAD_DIM, 1)
    return jnp.where(low, o0, o1) * (1.0 / den)


def _split_heads(k, v, masks):
    ks = [jnp.where(m, k, jnp.zeros_like(k)) for m in masks]
    vs = [jnp.where(m, v, jnp.ones_like(v)) for m in masks]
    return ks, vs


def _ctx_attn_kernel(qkv_ref, o_ref, *, n_seq, L):
    low = lax.broadcasted_iota(jnp.int32, (1, LANES), 1) < HEAD_DIM
    masks = (low, jnp.logical_not(low))

    def seq_body(s, carry):
        rows = pl.ds(pl.multiple_of(s * L, L), L)
        for p in range(ATTN_W // LANES):
            cols = lambda part: slice(part * ATTN_W + p * LANES, part * ATTN_W + (p + 1) * LANES)
            qb = qkv_ref[rows, cols(0)]
            kb = qkv_ref[rows, cols(1)]
            vb = qkv_ref[rows, cols(2)]
            outs = []
            for m in masks:
                s_ = _dot_t(qb, jnp.where(m, kb, jnp.zeros_like(kb)))
                p_ = jnp.exp(s_ - jnp.max(s_, axis=-1, keepdims=True))
                outs.append(_dot(p_.astype(BF16), vb) * (1.0 / jnp.sum(p_, axis=-1, keepdims=True)))
            o_ref[rows, p * LANES:(p + 1) * LANES] = jnp.where(low, outs[0], outs[1])
        return carry

    lax.fori_loop(0, n_seq, seq_body, 0, unroll=4)


def _ctx_attn(qkv, L, casts=()):
    n = qkv.shape[0]
    cast_specs, cast_shapes = _cast_specs(casts, n // ROW_BLOCK)
    return pl.pallas_call(
        _with_casts(functools.partial(_ctx_attn_kernel, n_seq=ROW_BLOCK // L, L=L), 1, 1, len(casts)),
        grid=(n // ROW_BLOCK,),
        in_specs=[pl.BlockSpec((ROW_BLOCK, 3 * ATTN_W), lambda i: (i, 0))] + cast_specs,
        out_specs=[pl.BlockSpec((ROW_BLOCK, ATTN_W), lambda i: (i, 0))] + cast_specs,
        out_shape=[jax.ShapeDtypeStruct((n, ATTN_W), F32)] + cast_shapes,
        compiler_params=_cparams(("arbitrary",)),
        name="ctx_attn",
    )(qkv, *casts)


N_DR = 2 * WIN_ROWS - 1
N_DC = 2 * WIN_COLS - 1


def _rpb_expand_table():
    d = np.clip(np.arange(LANES) - GRID_W, -(WIN_COLS - 1), WIN_COLS - 1) + (WIN_COLS - 1)
    t = np.zeros((32, LANES), np.float32)
    t[d, np.arange(LANES)] = 1.0
    return t


def _bias_tile_index(n_rows):
    n_blk = n_rows // Q_ROWS
    idx = np.full((n_blk, Q_ROWS, KEY_ROWS), N_DR, np.int32)
    for i in range(n_blk):
        ks = min(max(Q_ROWS * i - WIN_ROWS // 2, 0), n_rows - KEY_ROWS)
        for a in range(Q_ROWS):
            r = Q_ROWS * i + a
            rs = min(max(r - WIN_ROWS // 2, 0), n_rows - WIN_ROWS)
            for j in range(KEY_ROWS):
                if rs <= ks + j < rs + WIN_ROWS:
                    idx[i, a, j] = ks + j - r + (WIN_ROWS - 1)
            assert (idx[i, a] != N_DR).sum() == WIN_ROWS, "key window must cover the whole band"
    return idx


def _build_bias(rpb_ref, ext_ref, tw_ref, bias_ref, n_blk):
    shape = (GRID_W, LANES)
    lane = lax.broadcasted_iota(jnp.int32, shape, 1)
    qc = lax.broadcasted_iota(jnp.int32, shape, 0)
    kc = lane & (GRID_W - 1)
    low = lane < GRID_W
    col_start = jnp.clip(qc - WIN_COLS // 2, 0, GRID_W - WIN_COLS)
    col_in = (kc >= col_start) & (kc < col_start + WIN_COLS)
    idx = _bias_tile_index(Q_ROWS * n_blk)
    for e in range(2):
        g = jnp.dot(rpb_ref[e], ext_ref[...], precision=HIGHEST, preferred_element_type=F32)
        for dr in range(N_DR):
            t = pltpu.roll(jnp.broadcast_to(g[dr:dr + 1, :], shape), GRID_W, 1, stride=1, stride_axis=0)
            t = jnp.where(low, t, pltpu.roll(t, GRID_W, 1))
            tw_ref[e, dr] = jnp.where(col_in, t, NEG)
        tw_ref[e, N_DR] = jnp.full(shape, NEG, F32)
        for i in range(n_blk):
            for a in range(Q_ROWS):
                for jt in range(KEY_ROWS // 2):
                    tile = jnp.where(low, tw_ref[e, int(idx[i, a, 2 * jt])], tw_ref[e, int(idx[i, a, 2 * jt + 1])])
                    bias_ref[e, i, a * GRID_W:(a + 1) * GRID_W, jt * LANES:(jt + 1) * LANES] = tile


def _lat_attn_kernel(q_ref, k_ref, v_ref, kc_ref, vc_ref, rpb_ref, ext_ref, o_ref, km_ref, vm_ref, tw_ref,
                     bias_ref, *, n_blk):
    @pl.when(pl.program_id(1) == 0)
    def _():
        _build_bias(rpb_ref, ext_ref, tw_ref, bias_ref, n_blk)

    low = lax.broadcasted_iota(jnp.int32, (1, LANES), 1) < HEAD_DIM
    masks = (low, jnp.logical_not(low))
    ks, vs = _split_heads(k_ref[...], v_ref[...], masks)
    for e in range(2):
        km_ref[e] = ks[e]
        vm_ref[e] = vs[e]
    kce, vce = _split_heads(kc_ref[0], vc_ref[0], masks)
    nq = Q_ROWS * GRID_W
    nk = KEY_ROWS * GRID_W
    for i in range(n_blk):
        qrows = slice(i * nq, (i + 1) * nq)
        k0 = min(max(Q_ROWS * i - WIN_ROWS // 2, 0), Q_ROWS * n_blk - KEY_ROWS) * GRID_W
        krows = slice(k0, k0 + nk)
        qb = q_ref[qrows, :]
        outs = []
        for e in range(2):
            s_loc = _dot_t(qb, km_ref[e, krows, :]) + bias_ref[e, i]
            s_ctx = _dot_t(qb, kce[e])
            outs.append(_softmax_pv([s_loc, s_ctx], [vm_ref[e, krows, :], vce[e]]))
        o_ref[qrows, :] = _normalise_heads(outs[0], outs[1], low)


def _lat_attn(qkv, kc, vc, rpb, batch, L):
    n_blk = L // (Q_ROWS * GRID_W)
    n_pairs = ATTN_W // LANES
    col = lambda part: pl.BlockSpec((L, LANES), lambda p, b: (b, part * n_pairs + p))
    cache = pl.BlockSpec((1, kc.shape[1], LANES), lambda p, b: (b, 0, p))
    rpb_pad = jnp.pad(rpb, ((0, 0), (0, 16 - N_DR), (0, 32 - N_DC)))
    return pl.pallas_call(
        functools.partial(_lat_attn_kernel, n_blk=n_blk),
        grid=(n_pairs, batch),
        in_specs=[col(0), col(1), col(2), cache, cache,
                  pl.BlockSpec((2, 16, 32), lambda p, b: (p, 0, 0)),
                  pl.BlockSpec((32, LANES), lambda p, b: (0, 0))],
        out_specs=pl.BlockSpec((L, LANES), lambda p, b: (b, p)),
        out_shape=jax.ShapeDtypeStruct((batch * L, ATTN_W), F32),
        scratch_shapes=[pltpu.VMEM((2, L, LANES), BF16), pltpu.VMEM((2, L, LANES), BF16),
                        pltpu.VMEM((2, 16, GRID_W, LANES), F32),
                        pltpu.VMEM((2, n_blk, Q_ROWS * GRID_W, KEY_ROWS * GRID_W), F32)],
        compiler_params=_cparams(("arbitrary", "arbitrary")),
        name="lat_attn",
    )(qkv, qkv, qkv, kc, vc, rpb_pad, jnp.asarray(_rpb_expand_table()))


SUBLANES = 8


def _row_neighbours(x, L):
    n, c = x.shape
    t, tl = n // SUBLANES, L // SUBLANES
    x3 = x.reshape(t, SUBLANES, c)
    sub = lax.broadcasted_iota(jnp.int32, (1, SUBLANES, 1), 1)
    zero = jnp.zeros((1, SUBLANES, c), x.dtype)
    down = pltpu.roll(x3, 1, 1)
    up = pltpu.roll(x3, SUBLANES - 1, 1)
    down_prev = jnp.concatenate([p for s in range(0, t, tl) for p in (zero, down[s:s + tl - 1])], axis=0)
    up_next = jnp.concatenate([p for s in range(0, t, tl) for p in (up[s + 1:s + tl], zero)], axis=0)
    above = jnp.where(sub == 0, down_prev, down).reshape(n, c)
    below = jnp.where(sub == SUBLANES - 1, up_next, up).reshape(n, c)
    return above, below


def _dwconv3(x, w_ref, b_ref, L):
    above, below = _row_neighbours(x, L)
    return above * w_ref[0:1, :] + x * w_ref[1:2, :] + below * w_ref[2:3, :] + b_ref[...]


def _hyena_kernel(h1_ref, h2_ref, hv_ref, w1_ref, w2_ref, wv_ref, b1_ref, b2_ref, bv_ref,
                  kf_ref, fb_ref, a_ref, ai_ref, o_ref, *, L):
    row0 = lax.broadcasted_iota(jnp.int32, (L, 1), 0) == 0

    def long_conv(u, order):
        ub = u.astype(BF16)
        ur = _dot(a_ref[0:L, :], ub)
        ui = _dot(a_ref[L:2 * L, :], ub)
        kr = kf_ref[order, 0:L, :]
        ki = kf_ref[order, L:2 * L, :]
        yr = jnp.where(row0, ur * kr, ur * kr - ui * ki)
        yi = jnp.where(row0, ui * ki, ur * ki + ui * kr)
        y = _dot(ai_ref[:, 0:L], yr.astype(BF16)) + _dot(ai_ref[:, L:2 * L], yi.astype(BF16))
        return y + u * fb_ref[order:order + 1, :]

    def one_sequence(rows):
        v = _dwconv3(hv_ref[rows, :], wv_ref, bv_ref, L)
        z = _dwconv3(h1_ref[rows, :], w1_ref, b1_ref, L) * long_conv(v, 0)
        o_ref[rows, :] = _dwconv3(h2_ref[rows, :], w2_ref, b2_ref, L) * long_conv(z, 1)

    n_seq = h1_ref.shape[0] // L
    if n_seq == 1:
        one_sequence(slice(0, L))
    else:
        def seq_body(s, carry):
            one_sequence(pl.ds(pl.multiple_of(s * L, L), L))
            return carry

        lax.fori_loop(0, n_seq, seq_body, 0, unroll=2)


def _hyena(hy, conv_w, conv_b, kf, filt_bias, a_fwd, a_inv, L, casts=()):
    n = hy.shape[0]
    cast_specs, cast_shapes = _cast_specs(casts, n // ROW_BLOCK)
    part = lambda k, rows: pl.BlockSpec((rows, HY_CH), lambda i: (i if rows == ROW_BLOCK else 0, k))
    once = lambda shape: pl.BlockSpec(shape, lambda i: (0,) * len(shape), pipeline_mode=pl.Buffered(1))
    return pl.pallas_call(
        _with_casts(functools.partial(_hyena_kernel, L=L), 13, 1, len(casts)),
        grid=(n // ROW_BLOCK,),
        in_specs=[part(0, ROW_BLOCK), part(1, ROW_BLOCK), part(2, ROW_BLOCK),
                  part(0, 3), part(1, 3), part(2, 3),
                  part(0, 1), part(1, 1), part(2, 1),
                  once((2, 2 * L, HY_CH)), once((2, HY_CH)), once((2 * L, L)), once((L, 2 * L))] + cast_specs,
        out_specs=[pl.BlockSpec((ROW_BLOCK, HY_CH), lambda i: (i, 0))] + cast_specs,
        out_shape=[jax.ShapeDtypeStruct((n, HY_CH), F32)] + cast_shapes,
        compiler_params=_cparams(("arbitrary",)),
        name=f"hyena_{L}",
    )(hy, hy, hy, conv_w, conv_w, conv_w, conv_b, conv_b, conv_b, kf, filt_bias, a_fwd, a_inv, *casts)


def _tail_kernel(*refs, L, mod_base, blocks_per_mod, halo, n_blocks):
    rows_refs, rest = refs[:9 if halo else 3], refs[9 if halo else 3:]
    (mods_ref, gg_ref, wo_ref, n2_ref, fg_ref, wu_ref, cw_ref, cb_ref, wd_ref, o_ref,
     h2_s, x1_s, acc_s, ma_s, mb_s, mc_s) = rest
    rows = o_ref.shape[0]
    i = pl.program_id(0)
    ext = rows + 2 * halo
    base = TAIL_PAD + halo
    n_tiles, seq_tiles = rows // SUBLANES, L // SUBLANES
    last_tile = N_FF_TILES - 1
    sub = lax.broadcasted_iota(jnp.int32, (1, SUBLANES, 1), 1)
    zero_tile = jnp.zeros((1, SUBLANES, FF_TILE), F32)

    def mod(k, blk):
        return mods_ref[k, pl.ds(mod_base + blk // blocks_per_mod, 1), :]

    def up_proj(j, dst):
        for e in range(2):
            cols = pl.ds(pl.multiple_of(e * D_FF + j * FF_TILE, FF_TILE), FF_TILE)
            dst[e, TAIL_PAD:TAIL_PAD + ext, :] = _dot(h2_s[...], wu_ref[:, cols])

    def conv(src, e, j, blk):
        t = src[e, base - SUBLANES:base + rows + SUBLANES, :].reshape(n_tiles + 2, SUBLANES, FF_TILE)
        down = pltpu.roll(t, 1, 1)
        up = pltpu.roll(t, SUBLANES - 1, 1)
        if L >= rows:
            starts_seq = (blk * rows) & (L - 1) == 0
            ends_seq = ((blk + 1) * rows) & (L - 1) == 0
            above = [jnp.where(starts_seq, zero_tile, down[0:1]), down[1:n_tiles]]
            below = [up[2:n_tiles + 1], jnp.where(ends_seq, zero_tile, up[n_tiles + 1:n_tiles + 2])]
        else:
            above, below = [], []
            for s in range(0, n_tiles, seq_tiles):
                above += [down[0:1] if s == 0 else zero_tile, down[s + 1:s + seq_tiles]]
                last_seq = s + seq_tiles == n_tiles
                below += [up[s + 2:s + seq_tiles + 1], up[n_tiles + 1:n_tiles + 2] if last_seq else zero_tile]
        above = jnp.where(sub == 0, jnp.concatenate(above, axis=0), down[1:n_tiles + 1])
        below = jnp.where(sub == SUBLANES - 1, jnp.concatenate(below, axis=0), up[1:n_tiles + 1])
        w = cw_ref[e, j]
        return above * w[0:1] + t[1:n_tiles + 1] * w[1:2] + below * w[2:3] + cb_ref[e, j]

    def down_proj(j, src, blk):
        gate = conv(src, 0, j, blk)
        act = gate / (1.0 + jnp.exp(-gate)) * conv(src, 1, j, blk)
        return _dot(act.reshape(rows, FF_TILE).astype(BF16), wd_ref[j])

    def start_block(blk):
        if halo:
            x_ref, a_ref, hy_ref, xp_ref, ap_ref, hp_ref, xn_ref, an_ref, hn_ref = rows_refs
            cat = lambda p, m, n: jnp.concatenate([p[...], m[...], n[...]], axis=0)
            x, a, hy = cat(xp_ref, x_ref, xn_ref), cat(ap_ref, a_ref, an_ref), cat(hp_ref, hy_ref, hn_ref)
        else:
            x, a, hy = (ref[...] for ref in rows_refs)
        gg = gg_ref[...]
        merged = jnp.concatenate([_rms(a) * gg[:, :ATTN_W], _rms(hy) * gg[:, ATTN_W:]], axis=-1).astype(BF16)
        x1 = x + mod(2, blk) * _dot(merged, wo_ref[...])
        h2_s[...] = (_rms(x1) * n2_ref[...] * (1.0 + mod(4, blk)) + mod(3, blk)).astype(BF16)
        x1_s[...] = x1[halo:halo + rows]
        up_proj(0, mc_s)
        up_proj(1, mb_s)
        acc_s[...] = down_proj(0, mc_s, blk)
        up_proj(2, ma_s)
        acc_s[...] += down_proj(1, mb_s, blk)

    def finish_block(blk):
        x2 = x1_s[...] + mod(5, blk) * (acc_s[...] + down_proj(last_tile, ma_s, blk))
        o_ref[...] = _rms(x2) * fg_ref[...]

    @pl.when(i == 0)
    def _():
        for buf in (ma_s, mb_s, mc_s):
            for e in range(2):
                buf[e, 0:TAIL_PAD, :] = jnp.zeros((TAIL_PAD, FF_TILE), F32)
                buf[e, TAIL_PAD + ext:, :] = jnp.zeros((TAIL_PAD, FF_TILE), F32)
        start_block(i)

    @pl.when(jnp.logical_and(i > 0, i < n_blocks))
    def _():
        finish_block(i - 1)
        start_block(i)

    @pl.when(i == n_blocks)
    def _():
        finish_block(i - 1)

    @pl.when(i < n_blocks)
    def _():
        def two_tiles(k, carry):
            j = 2 * k
            up_proj(j + 1, mb_s)
            acc_s[...] += down_proj(j, ma_s, i)
            up_proj(j + 2, ma_s)
            acc_s[...] += down_proj(j + 1, mb_s, i)
            return carry

        lax.fori_loop(1, (N_FF_TILES - 1) // 2, two_tiles, 0)


def _tail(x, a, hyo, mods, grp_g, w_out, n2_g, final_g, w_up, conv_w, conv_b, w_down, *, L, mod_base,
          blocks_per_mod):
    n = x.shape[0]
    n_blocks = n // TAIL_ROWS
    halo = TAIL_HALO if L > TAIL_ROWS else 0
    cur = lambda i: jnp.minimum(i, n_blocks - 1)
    row = lambda w: pl.BlockSpec((TAIL_ROWS, w), lambda i: (cur(i), 0))
    per_halo = TAIL_ROWS // TAIL_HALO
    prev = lambda w: pl.BlockSpec((TAIL_HALO, w), lambda i: (jnp.maximum(cur(i) * per_halo - 1, 0), 0))
    nxt = lambda w: pl.BlockSpec(
        (TAIL_HALO, w), lambda i: (jnp.minimum((cur(i) + 1) * per_halo, n // TAIL_HALO - 1), 0))
    once = lambda shape: pl.BlockSpec(shape, lambda i: (0,) * len(shape), pipeline_mode=pl.Buffered(1))
    widths = (D_MODEL, ATTN_W, HY_CH)
    row_specs = [row(w) for w in widths]
    row_args = [x, a, hyo]
    if halo:
        row_specs += [prev(w) for w in widths] + [nxt(w) for w in widths]
        row_args += [x, a, hyo, x, a, hyo]
    ext = TAIL_ROWS + 2 * halo
    staging = pltpu.VMEM((2, ext + 2 * TAIL_PAD, FF_TILE), F32)
    return pl.pallas_call(
        functools.partial(_tail_kernel, L=L, mod_base=mod_base, blocks_per_mod=blocks_per_mod, halo=halo,
                          n_blocks=n_blocks),
        grid=(n_blocks + 1,),
        in_specs=row_specs + [
            once((6, 8, D_MODEL)), once((1, D_MODEL)), once((D_MODEL, D_MODEL)), once((1, D_MODEL)),
            once((1, D_MODEL)),
            once((D_MODEL, 2 * D_FF)),
            once((2, N_FF_TILES, 3, FF_TILE)), once((2, N_FF_TILES, 1, FF_TILE)),
            once((N_FF_TILES, FF_TILE, D_MODEL))],
        out_specs=pl.BlockSpec((TAIL_ROWS, D_MODEL), lambda i: (jnp.maximum(i - 1, 0), 0)),
        out_shape=jax.ShapeDtypeStruct((n, D_MODEL), F32),
        scratch_shapes=[pltpu.VMEM((ext, D_MODEL), BF16), pltpu.VMEM((TAIL_ROWS, D_MODEL), F32),
                        pltpu.VMEM((TAIL_ROWS, D_MODEL), F32), staging, staging, staging],
        compiler_params=_cparams(("arbitrary",)),
        name=f"tail_{L}",
    )(*row_args, mods, grp_g, w_out, n2_g, final_g, w_up, conv_w, conv_b, w_down)


def kernel(x_prompt, x_sample, cache_ctx_k, cache_ctx_v, c, c_ctx, w_ada, b_ada, norm1_g, w_in, rpb,
           hy_conv_w, hy_conv_b, filt_w1, filt_b1, filt_w2, filt_b2, filt_w3, filt_b3, filt_freq,
           filt_bias, grp_norm_g, w_out, norm2_g, w_up, ffn_conv_w, ffn_conv_b, w_down, final_g):
    assert w_ada.shape[0] == 1, "single-layer trunk"
    bc, lc, _ = x_prompt.shape
    bl, ll, _ = x_sample.shape
    past = cache_ctx_k.shape[3]

    cvec = jnp.concatenate([c_ctx[None], c, jnp.zeros((8 - 1 - bl, D_MODEL), F32)], axis=0)
    mods = _mods(cvec, w_ada[0], b_ada[0])

    conv_w_t = ffn_conv_w[0].reshape(3, 2, N_FF_TILES, FF_TILE).transpose(1, 2, 0, 3)
    conv_b_t = ffn_conv_b[0].reshape(2, N_FF_TILES, 1, FF_TILE)
    g1 = norm1_g[0][None]
    filt = (filt_w1[0], filt_b1[0], filt_w2[0], filt_b2[0], filt_w3[0], filt_b3[0], filt_freq[0])

    def tables(L):
        fwd, inv = _dft_tables(L)
        return jnp.asarray(fwd).astype(BF16), jnp.asarray(inv).astype(BF16)

    xc = x_prompt.reshape(bc * lc, D_MODEL)
    a_fwd, a_inv = tables(lc)
    kf, w_in_b = _filter_spectrum(lc, a_fwd, *filt, casts=(w_in[0],))
    qkv, hy, state_k, state_v = _proj(xc, mods, g1, w_in_b, mod_base=0, blocks_per_mod=bc * lc // PROJ_ROWS,
                                      L=lc, emit_state=True)
    att, w_up_b = _ctx_attn(qkv, lc, casts=(w_up[0],))
    att, xs = lax.optimization_barrier((att, x_sample.reshape(bl * ll, D_MODEL)))
    hyo, w_down_b, w_out_b = _hyena(hy, hy_conv_w[0], hy_conv_b[0][None], kf, filt_bias[0], a_fwd, a_inv, lc,
                                    casts=(w_down[0], w_out[0]))
    shared_tail = (grp_norm_g[0][None], w_out_b, norm2_g[0][None], final_g[None], w_up_b, conv_w_t, conv_b_t,
                   w_down_b.reshape(N_FF_TILES, FF_TILE, D_MODEL))
    y_prompt = _tail(xc, att, hyo, mods, *shared_tail, L=lc, mod_base=0,
                     blocks_per_mod=bc * lc // TAIL_ROWS).reshape(bc, lc, D_MODEL)

    a_fwd, a_inv = tables(ll)
    kf, = _filter_spectrum(ll, a_fwd, *filt)
    qkv, hy = _proj(xs, mods, g1, w_in_b, mod_base=1, blocks_per_mod=ll // PROJ_ROWS, L=ll, emit_state=False)
    lanes_major = lambda t: t[:, 0].transpose(0, 2, 1, 3).reshape(bl, past, ATTN_W).astype(BF16)
    hyo, = _hyena(hy, hy_conv_w[0], hy_conv_b[0][None], kf, filt_bias[0], a_fwd, a_inv, ll)
    hyo, qkv = lax.optimization_barrier((hyo, qkv))
    att = _lat_attn(qkv, lanes_major(cache_ctx_k), lanes_major(cache_ctx_v), rpb[0], bl, ll)
    y_sample = _tail(xs, att, hyo, mods, *shared_tail, L=ll, mod_base=1,
                     blocks_per_mod=ll // TAIL_ROWS).reshape(bl, ll, D_MODEL)

    return (y_prompt, y_sample, jnp.swapaxes(state_k, 3, 4), jnp.swapaxes(state_v, 3, 4))
```

```python
import functools
import math

import numpy as np
import jax
import jax.numpy as jnp
from jax import lax
from jax.experimental import pallas as pl
from jax.experimental.pallas import tpu as pltpu

F32 = jnp.float32
BF16 = jnp.bfloat16
HIGHEST = lax.Precision.HIGHEST

D_MODEL = 1024
N_HEADS = 8
HEAD_DIM = 64
ATTN_W = N_HEADS * HEAD_DIM
HY_CH = D_MODEL - ATTN_W
IN_COLS = 3 * ATTN_W + 3 * HY_CH
D_FF = 2816
GRID_W = 64
WIN_ROWS = 8
WIN_COLS = 16
FILT_FREQS = 8
FILT_HID = 64
FILT_EMB_PAD = 32
DECAY_TARGET = 1e-2
MAX_DECAY = math.log(DECAY_TARGET) / 0.3
MIN_DECAY = math.log(DECAY_TARGET) / 1.5
EPS = 1e-6
NEG = -1e30

LANES = 128
ROW_BLOCK = 1024
PROJ_ROWS = 1024
TAIL_ROWS = 512
TAIL_HALO = 16
TAIL_PAD = 8
FF_TILE = 256
N_FF_TILES = D_FF // FF_TILE
assert N_FF_TILES % 2 == 1 and N_FF_TILES >= 3, "the ConvFFN pipeline peels an odd tile count"
Q_ROWS = 4
KEY_ROWS = 12
VMEM_LIMIT = 56 * 1024 * 1024


def _cparams(sem):
    return pltpu.CompilerParams(dimension_semantics=sem, vmem_limit_bytes=VMEM_LIMIT)


def _rms(x):
    return x * lax.rsqrt(jnp.mean(x * x, axis=-1, keepdims=True) + EPS)


def _dot(a, b):
    return jnp.dot(a, b, preferred_element_type=F32)


def _dot3(a, b):
    ah = a.astype(BF16)
    bh = b.astype(BF16)
    al = (a - ah.astype(F32)).astype(BF16)
    bl = (b - bh.astype(F32)).astype(BF16)
    return _dot(ah, bh) + _dot(al, bh) + _dot(ah, bl)


def _dot_t(a, b):
    return lax.dot_general(a, b, (((1,), (1,)), ((), ())), preferred_element_type=F32)


def _with_casts(body, n_in, n_out, n_cast):
    def kernel(*refs):
        a, b, c = n_in + n_cast, n_in + n_cast + n_out, n_in + 2 * n_cast + n_out
        for src, dst in zip(refs[n_in:a], refs[b:c]):
            dst[...] = src[...].astype(BF16)
        body(*refs[:n_in], *refs[a:b], *refs[c:])

    return kernel


def _cast_specs(casts, n_steps):
    specs = [pl.BlockSpec((w.shape[0] // n_steps, w.shape[1]), lambda i: (i, 0)) for w in casts]
    return specs, [jax.ShapeDtypeStruct(w.shape, BF16) for w in casts]


def _mods_kernel(c_ref, w_ref, b_ref, o_ref):
    cv = c_ref[...]
    s = cv / (1.0 + jnp.exp(-cv))
    o_ref[0] = _dot3(s, w_ref[...]) + b_ref[0]


def _mods(cvec, w_ada, b_ada):
    return pl.pallas_call(
        _mods_kernel,
        grid=(6,),
        in_specs=[
            pl.BlockSpec((8, D_MODEL), lambda j: (0, 0)),
            pl.BlockSpec((D_MODEL, D_MODEL), lambda j: (0, j)),
            pl.BlockSpec((1, 1, D_MODEL), lambda j: (j, 0, 0)),
        ],
        out_specs=pl.BlockSpec((1, 8, D_MODEL), lambda j: (j, 0, 0)),
        out_shape=jax.ShapeDtypeStruct((6, 8, D_MODEL), F32),
        compiler_params=_cparams(("arbitrary",)),
        name="mods",
    )(cvec, w_ada, b_ada.reshape(6, 1, D_MODEL))


def _dft_tables(L):
    f = np.arange(L, dtype=np.int64)[:, None]
    t = np.arange(L, dtype=np.int64)[None, :]
    ang = np.pi * ((f * t) % (2 * L)).astype(np.float64) / L
    top = np.cos(ang)
    bot = -np.sin(ang)
    bot[0, :] = np.where(np.arange(L) % 2 == 0, 1.0, -1.0)
    fwd = np.concatenate([top, bot], axis=0)
    w = np.full((2 * L,), 1.0 / L)
    w[0] = w[L] = 0.5 / L
    inv = (fwd * w[:, None]).T
    return fwd.astype(np.float32), inv.astype(np.float32)


def _filter_consts(L):
    t = np.arange(L, dtype=np.float64) / L
    fr = np.arange(1, FILT_FREQS + 1, dtype=np.float64)
    ang = 2.0 * math.pi * fr[:, None] * t[None, :]
    z = np.zeros((FILT_EMB_PAD, L), np.float64)
    z[0] = t
    z[1:1 + FILT_FREQS] = np.cos(ang)
    z[1 + FILT_FREQS:1 + 2 * FILT_FREQS] = np.sin(ang)
    deltas = np.abs(np.linspace(MIN_DECAY, MAX_DECAY, HY_CH))
    decay = np.exp(-t[:, None] * deltas[None, :])
    return z.astype(np.float32), decay.astype(np.float32)


def _filter_kernel(z_ref, w1_ref, b1_ref, w2_ref, b2_ref, w3_ref, b3_ref, fr_ref, dec_ref,
                   a_ref, o_ref, h_s, *, L):
    @pl.when(pl.program_id(0) == 0)
    def _():
        fr = fr_ref[...]
        h = jnp.sin(fr * (jnp.dot(w1_ref[...], z_ref[...], precision=HIGHEST,
                                  preferred_element_type=F32) + b1_ref[...]))
        h_s[...] = jnp.sin(fr * (jnp.dot(w2_ref[...], h, precision=HIGHEST,
                                         preferred_element_type=F32) + b2_ref[...]))

    tdot = lambda a, b: lax.dot_general(a, b, (((0,), (0,)), ((), ())), preferred_element_type=F32)
    hid, w3 = h_s[...], w3_ref[...]
    hh, wh = hid.astype(BF16), w3.astype(BF16)
    hl, wl = (hid - hh.astype(F32)).astype(BF16), (w3 - wh.astype(F32)).astype(BF16)
    h = tdot(hh, wh) + tdot(hl, wh) + tdot(hh, wl) + b3_ref[...]
    dec = dec_ref[...]
    row0 = lax.broadcasted_iota(jnp.int32, (L, 1), 0) == 0
    fwd = h[:, :HY_CH] * dec
    bwd = jnp.where(row0, 0.0, h[:, HY_CH:] * dec)
    gp = (fwd + bwd).astype(BF16)
    gm = (fwd - bwd).astype(BF16)
    top = _dot(a_ref[0:L, :], gp)
    bot = _dot(a_ref[L:2 * L, :], gm)
    nyq = _dot(a_ref[L:L + 16, :], gp)[0:1]
    o_ref[0, 0:L, :] = top
    o_ref[0, L:2 * L, :] = jnp.where(row0, nyq, bot)


def _filter_spectrum(L, a_fwd, w1, b1, w2, b2, w3, b3, freq, casts=()):
    z, decay = _filter_consts(L)
    full = lambda shape: pl.BlockSpec(shape, lambda o: (0,) * len(shape))
    col = lambda v: v[:, None]
    cast_specs, cast_shapes = _cast_specs(casts, 2)
    return pl.pallas_call(
        _with_casts(functools.partial(_filter_kernel, L=L), 10, 1, len(casts)),
        grid=(2,),
        in_specs=[
            full((FILT_EMB_PAD, L)),
            full((FILT_HID, FILT_EMB_PAD)),
            full((FILT_HID, 1)),
            full((FILT_HID, FILT_HID)),
            full((FILT_HID, 1)),
            pl.BlockSpec((FILT_HID, 2 * HY_CH), lambda o: (0, o)),
            pl.BlockSpec((1, 2 * HY_CH), lambda o: (0, o)),
            full((FILT_HID, 1)),
            full((L, HY_CH)),
            full((2 * L, L)),
        ] + cast_specs,
        out_specs=[pl.BlockSpec((1, 2 * L, HY_CH), lambda o: (o, 0, 0))] + cast_specs,
        out_shape=[jax.ShapeDtypeStruct((2, 2 * L, HY_CH), F32)] + cast_shapes,
        scratch_shapes=[pltpu.VMEM((FILT_HID, L), F32)],
        compiler_params=_cparams(("arbitrary",)),
        name=f"filter_{L}",
    )(jnp.asarray(z), jnp.pad(w1, ((0, FILT_EMB_PAD - w1.shape[0]), (0, 0))).T, col(b1), w2.T, col(b2),
      w3, b3[None], col(freq), jnp.asarray(decay), a_fwd, *casts)


def _store_heads_major(x, dst_ref, L):
    xt = x.T
    for s in range(x.shape[0] // L):
        for h in range(N_HEADS):
            dst_ref[s, 0, h] = xt[h * HEAD_DIM:(h + 1) * HEAD_DIM, s * L:(s + 1) * L]


def _proj_kernel(x_ref, mods_ref, g_ref, w_ref, qkv_ref, hy_ref, *state, mod_base, blocks_per_mod, L):
    r = mod_base + pl.program_id(0) // blocks_per_mod
    sh1 = mods_ref[0, pl.ds(r, 1), :]
    sc1 = mods_ref[1, pl.ds(r, 1), :]
    h = (_rms(x_ref[...]) * g_ref[...] * (1.0 + sc1) + sh1).astype(BF16)
    q = _dot(h, w_ref[:, 0:ATTN_W])
    qkv_ref[:, 0:ATTN_W] = (q * HEAD_DIM ** -0.5).astype(BF16)
    k = _dot(h, w_ref[:, ATTN_W:2 * ATTN_W])
    qkv_ref[:, ATTN_W:2 * ATTN_W] = k.astype(BF16)
    v = _dot(h, w_ref[:, 2 * ATTN_W:3 * ATTN_W])
    qkv_ref[:, 2 * ATTN_W:3 * ATTN_W] = v.astype(BF16)
    if state:
        ks_ref, vs_ref = state
        _store_heads_major(k, ks_ref, L)
        _store_heads_major(v, vs_ref, L)
    hy_ref[...] = _dot(h, w_ref[:, 3 * ATTN_W:])


def _proj(x, mods, g, w_in, *, mod_base, blocks_per_mod, L, emit_state):
    n = x.shape[0]
    row = lambda w: pl.BlockSpec((PROJ_ROWS, w), lambda i: (i, 0))
    out_specs = [row(3 * ATTN_W), row(3 * HY_CH)]
    out_shape = [jax.ShapeDtypeStruct((n, 3 * ATTN_W), BF16), jax.ShapeDtypeStruct((n, 3 * HY_CH), F32)]
    if emit_state:
        seqs = PROJ_ROWS // L
        state = pl.BlockSpec((seqs, 1, N_HEADS, HEAD_DIM, L), lambda i: (i, 0, 0, 0, 0))
        out_specs += [state, state]
        out_shape += [jax.ShapeDtypeStruct((n // L, 1, N_HEADS, HEAD_DIM, L), F32)] * 2
    return pl.pallas_call(
        functools.partial(_proj_kernel, mod_base=mod_base, blocks_per_mod=blocks_per_mod, L=L),
        grid=(n // PROJ_ROWS,),
        in_specs=[
            row(D_MODEL),
            pl.BlockSpec((6, 8, D_MODEL), lambda i: (0, 0, 0)),
            pl.BlockSpec((1, D_MODEL), lambda i: (0, 0)),
            pl.BlockSpec((D_MODEL, IN_COLS), lambda i: (0, 0), pipeline_mode=pl.Buffered(1)),
        ],
        out_specs=out_specs,
        out_shape=out_shape,
        compiler_params=_cparams(("arbitrary",)),
        name="proj_ctx" if emit_state else "proj_lat",
    )(x, mods, g, w_in)


def _softmax_pv(scores, values):
    mx = functools.reduce(jnp.maximum, [jnp.max(s, axis=-1, keepdims=True) for s in scores])
    return functools.reduce(jnp.add, [_dot(jnp.exp(s - mx).astype(BF16), v) for s, v in zip(scores, values)])


def _normalise_heads(o0, o1, low):
    den = pltpu.roll(jnp.where(low, o1, o0), HEAD_DIM, 1)
    return jnp.where(low, o0, o1) * (1.0 / den)


def _split_heads(k, v, masks):
    ks = [jnp.where(m, k, jnp.zeros_like(k)) for m in masks]
    vs = [jnp.where(m, v, jnp.ones_like(v)) for m in masks]
    return ks, vs


def _ctx_attn_kernel(qkv_ref, o_ref, *, n_seq, L):
    low = lax.broadcasted_iota(jnp.int32, (1, LANES), 1) < HEAD_DIM
    masks = (low, jnp.logical_not(low))

    def seq_body(s, carry):
        rows = pl.ds(pl.multiple_of(s * L, L), L)
        for p in range(ATTN_W // LANES):
            cols = lambda part: slice(part * ATTN_W + p * LANES, part * ATTN_W + (p + 1) * LANES)
            qb = qkv_ref[rows, cols(0)]
            kb = qkv_ref[rows, cols(1)]
            vb = qkv_ref[rows, cols(2)]
            outs = []
            for m in masks:
                s_ = _dot_t(qb, jnp.where(m, kb, jnp.zeros_like(kb)))
                p_ = jnp.exp(s_ - jnp.max(s_, axis=-1, keepdims=True))
                outs.append(_dot(p_.astype(BF16), vb) * (1.0 / jnp.sum(p_, axis=-1, keepdims=True)))
            o_ref[rows, p * LANES:(p + 1) * LANES] = jnp.where(low, outs[0], outs[1])
        return carry

    lax.fori_loop(0, n_seq, seq_body, 0, unroll=4)


def _ctx_attn(qkv, L, casts=()):
    n = qkv.shape[0]
    cast_specs, cast_shapes = _cast_specs(casts, n // ROW_BLOCK)
    return pl.pallas_call(
        _with_casts(functools.partial(_ctx_attn_kernel, n_seq=ROW_BLOCK // L, L=L), 1, 1, len(casts)),
        grid=(n // ROW_BLOCK,),
        in_specs=[pl.BlockSpec((ROW_BLOCK, 3 * ATTN_W), lambda i: (i, 0))] + cast_specs,
        out_specs=[pl.BlockSpec((ROW_BLOCK, ATTN_W), lambda i: (i, 0))] + cast_specs,
        out_shape=[jax.ShapeDtypeStruct((n, ATTN_W), F32)] + cast_shapes,
        compiler_params=_cparams(("arbitrary",)),
        name="ctx_attn",
    )(qkv, *casts)


N_DR = 2 * WIN_ROWS - 1
N_DC = 2 * WIN_COLS - 1


def _rpb_expand_table():
    d = np.clip(np.arange(LANES) - GRID_W, -(WIN_COLS - 1), WIN_COLS - 1) + (WIN_COLS - 1)
    t = np.zeros((32, LANES), np.float32)
    t[d, np.arange(LANES)] = 1.0
    return t


def _bias_tile_index(n_rows):
    n_blk = n_rows // Q_ROWS
    idx = np.full((n_blk, Q_ROWS, KEY_ROWS), N_DR, np.int32)
    for i in range(n_blk):
        ks = min(max(Q_ROWS * i - WIN_ROWS // 2, 0), n_rows - KEY_ROWS)
        for a in range(Q_ROWS):
            r = Q_ROWS * i + a
            rs = min(max(r - WIN_ROWS // 2, 0), n_rows - WIN_ROWS)
            for j in range(KEY_ROWS):
                if rs <= ks + j < rs + WIN_ROWS:
                    idx[i, a, j] = ks + j - r + (WIN_ROWS - 1)
            assert (idx[i, a] != N_DR).sum() == WIN_ROWS, "key window must cover the whole band"
    return idx


def _build_bias(rpb_ref, ext_ref, tw_ref, bias_ref, n_blk):
    shape = (GRID_W, LANES)
    lane = lax.broadcasted_iota(jnp.int32, shape, 1)
    qc = lax.broadcasted_iota(jnp.int32, shape, 0)
    kc = lane & (GRID_W - 1)
    low = lane < GRID_W
    col_start = jnp.clip(qc - WIN_COLS // 2, 0, GRID_W - WIN_COLS)
    col_in = (kc >= col_start) & (kc < col_start + WIN_COLS)
    idx = _bias_tile_index(Q_ROWS * n_blk)
    for e in range(2):
        g = jnp.dot(rpb_ref[e], ext_ref[...], precision=HIGHEST, preferred_element_type=F32)
        for dr in range(N_DR):
            t = pltpu.roll(jnp.broadcast_to(g[dr:dr + 1, :], shape), GRID_W, 1, stride=1, stride_axis=0)
            t = jnp.where(low, t, pltpu.roll(t, GRID_W, 1))
            tw_ref[e, dr] = jnp.where(col_in, t, NEG)
        tw_ref[e, N_DR] = jnp.full(shape, NEG, F32)
        for i in range(n_blk):
            for a in range(Q_ROWS):
                for jt in range(KEY_ROWS // 2):
                    tile = jnp.where(low, tw_ref[e, int(idx[i, a, 2 * jt])], tw_ref[e, int(idx[i, a, 2 * jt + 1])])
                    bias_ref[e, i, a * GRID_W:(a + 1) * GRID_W, jt * LANES:(jt + 1) * LANES] = tile


def _lat_attn_kernel(q_ref, k_ref, v_ref, kc_ref, vc_ref, rpb_ref, ext_ref, o_ref, km_ref, vm_ref, tw_ref,
                     bias_ref, *, n_blk, n_batch):
    @pl.when(pl.program_id(1) == 0)
    def _():
        _build_bias(rpb_ref, ext_ref, tw_ref, bias_ref, n_blk)

    low = lax.broadcasted_iota(jnp.int32, (1, LANES), 1) < HEAD_DIM
    masks = (low, jnp.logical_not(low))
    ks, vs = _split_heads(k_ref[...], v_ref[...], masks)
    for e in range(2):
        km_ref[e] = ks[e]
        vm_ref[e] = vs[e]
    nq = Q_ROWS * GRID_W
    nk = KEY_ROWS * GRID_W
    seq = n_blk * nq
    for b in range(n_batch):
        kce, vce = _split_heads(kc_ref[b], vc_ref[b], masks)
        for i in range(n_blk):
            qrows = slice(b * seq + i * nq, b * seq + (i + 1) * nq)
            k0 = b * seq + min(max(Q_ROWS * i - WIN_ROWS // 2, 0), Q_ROWS * n_blk - KEY_ROWS) * GRID_W
            krows = slice(k0, k0 + nk)
            qb = q_ref[qrows, :]
            outs = []
            for e in range(2):
                s_loc = _dot_t(qb, km_ref[e, krows, :]) + bias_ref[e, i]
                s_ctx = _dot_t(qb, kce[e])
                outs.append(_softmax_pv([s_loc, s_ctx], [vm_ref[e, krows, :], vce[e]]))
            o_ref[qrows, :] = _normalise_heads(outs[0], outs[1], low)


LAT_BATCH = 2


def _lat_attn(qkv, kc, vc, rpb, batch, L):
    n_blk = L // (Q_ROWS * GRID_W)
    n_pairs = ATTN_W // LANES
    rows = LAT_BATCH * L
    col = lambda part: pl.BlockSpec((rows, LANES), lambda p, b: (b, part * n_pairs + p))
    cache = pl.BlockSpec((LAT_BATCH, kc.shape[1], LANES), lambda p, b: (b, 0, p))
    rpb_pad = jnp.pad(rpb, ((0, 0), (0, 16 - N_DR), (0, 32 - N_DC)))
    return pl.pallas_call(
        functools.partial(_lat_attn_kernel, n_blk=n_blk, n_batch=LAT_BATCH),
        grid=(n_pairs, batch // LAT_BATCH),
        in_specs=[col(0), col(1), col(2), cache, cache,
                  pl.BlockSpec((2, 16, 32), lambda p, b: (p, 0, 0)),
                  pl.BlockSpec((32, LANES), lambda p, b: (0, 0))],
        out_specs=pl.BlockSpec((rows, LANES), lambda p, b: (b, p)),
        out_shape=jax.ShapeDtypeStruct((batch * L, ATTN_W), F32),
        scratch_shapes=[pltpu.VMEM((2, rows, LANES), BF16), pltpu.VMEM((2, rows, LANES), BF16),
                        pltpu.VMEM((2, 16, GRID_W, LANES), F32),
                        pltpu.VMEM((2, n_blk, Q_ROWS * GRID_W, KEY_ROWS * GRID_W), F32)],
        compiler_params=_cparams(("arbitrary", "arbitrary")),
        name="lat_attn",
    )(qkv, qkv, qkv, kc, vc, rpb_pad, jnp.asarray(_rpb_expand_table()))


SUBLANES = 8


def _row_neighbours(x, L):
    n, c = x.shape
    t, tl = n // SUBLANES, L // SUBLANES
    x3 = x.reshape(t, SUBLANES, c)
    sub = lax.broadcasted_iota(jnp.int32, (1, SUBLANES, 1), 1)
    zero = jnp.zeros((1, SUBLANES, c), x.dtype)
    down = pltpu.roll(x3, 1, 1)
    up = pltpu.roll(x3, SUBLANES - 1, 1)
    down_prev = jnp.concatenate([p for s in range(0, t, tl) for p in (zero, down[s:s + tl - 1])], axis=0)
    up_next = jnp.concatenate([p for s in range(0, t, tl) for p in (up[s + 1:s + tl], zero)], axis=0)
    above = jnp.where(sub == 0, down_prev, down).reshape(n, c)
    below = jnp.where(sub == SUBLANES - 1, up_next, up).reshape(n, c)
    return above, below


def _dwconv3(x, w_ref, b_ref, L):
    above, below = _row_neighbours(x, L)
    return above * w_ref[0:1, :] + x * w_ref[1:2, :] + below * w_ref[2:3, :] + b_ref[...]


def _hyena_kernel(h1_ref, h2_ref, hv_ref, w1_ref, w2_ref, wv_ref, b1_ref, b2_ref, bv_ref,
                  kf_ref, fb_ref, a_ref, ai_ref, o_ref, *, L):
    row0 = lax.broadcasted_iota(jnp.int32, (L, 1), 0) == 0

    def long_conv(u, order):
        ub = u.astype(BF16)
        ur = _dot(a_ref[0:L, :], ub)
        ui = _dot(a_ref[L:2 * L, :], ub)
        kr = kf_ref[order, 0:L, :]
        ki = kf_ref[order, L:2 * L, :]
        yr = jnp.where(row0, ur * kr, ur * kr - ui * ki)
        yi = jnp.where(row0, ui * ki, ur * ki + ui * kr)
        y = _dot(ai_ref[:, 0:L], yr.astype(BF16)) + _dot(ai_ref[:, L:2 * L], yi.astype(BF16))
        return y + u * fb_ref[order:order + 1, :]

    def one_sequence(rows):
        v = _dwconv3(hv_ref[rows, :], wv_ref, bv_ref, L)
        z = _dwconv3(h1_ref[rows, :], w1_ref, b1_ref, L) * long_conv(v, 0)
        o_ref[rows, :] = _dwconv3(h2_ref[rows, :], w2_ref, b2_ref, L) * long_conv(z, 1)

    n_seq = h1_ref.shape[0] // L
    if n_seq == 1:
        one_sequence(slice(0, L))
    else:
        def seq_body(s, carry):
            one_sequence(pl.ds(pl.multiple_of(s * L, L), L))
            return carry

        lax.fori_loop(0, n_seq, seq_body, 0, unroll=2)


def _hyena(hy, conv_w, conv_b, kf, filt_bias, a_fwd, a_inv, L, casts=()):
    n = hy.shape[0]
    cast_specs, cast_shapes = _cast_specs(casts, n // ROW_BLOCK)
    part = lambda k, rows: pl.BlockSpec((rows, HY_CH), lambda i: (i if rows == ROW_BLOCK else 0, k))
    once = lambda shape: pl.BlockSpec(shape, lambda i: (0,) * len(shape), pipeline_mode=pl.Buffered(1))
    return pl.pallas_call(
        _with_casts(functools.partial(_hyena_kernel, L=L), 13, 1, len(casts)),
        grid=(n // ROW_BLOCK,),
        in_specs=[part(0, ROW_BLOCK), part(1, ROW_BLOCK), part(2, ROW_BLOCK),
                  part(0, 3), part(1, 3), part(2, 3),
                  part(0, 1), part(1, 1), part(2, 1),
                  once((2, 2 * L, HY_CH)), once((2, HY_CH)), once((2 * L, L)), once((L, 2 * L))] + cast_specs,
        out_specs=[pl.BlockSpec((ROW_BLOCK, HY_CH), lambda i: (i, 0))] + cast_specs,
        out_shape=[jax.ShapeDtypeStruct((n, HY_CH), F32)] + cast_shapes,
        compiler_params=_cparams(("arbitrary",)),
        name=f"hyena_{L}",
    )(hy, hy, hy, conv_w, conv_w, conv_w, conv_b, conv_b, conv_b, kf, filt_bias, a_fwd, a_inv, *casts)


def _tail_kernel(*refs, L, mod_base, blocks_per_mod, halo, n_blocks):
    rows_refs, rest = refs[:9 if halo else 3], refs[9 if halo else 3:]
    (mods_ref, gg_ref, wo_ref, n2_ref, fg_ref, wu_ref, cw_ref, cb_ref, wd_ref, o_ref,
     h2_s, x1_s, acc_s, ma_s, mb_s, mc_s) = rest
    rows = o_ref.shape[0]
    i = pl.program_id(0)
    ext = rows + 2 * halo
    base = TAIL_PAD + halo
    n_tiles, seq_tiles = rows // SUBLANES, L // SUBLANES
    last_tile = N_FF_TILES - 1
    sub = lax.broadcasted_iota(jnp.int32, (1, SUBLANES, 1), 1)
    zero_tile = jnp.zeros((1, SUBLANES, FF_TILE), F32)

    def mod(k, blk):
        return mods_ref[k, pl.ds(mod_base + blk // blocks_per_mod, 1), :]

    def up_proj(j, dst):
        for e in range(2):
            cols = pl.ds(pl.multiple_of(e * D_FF + j * FF_TILE, FF_TILE), FF_TILE)
            dst[e, TAIL_PAD:TAIL_PAD + ext, :] = _dot(h2_s[...], wu_ref[:, cols])

    def conv(src, e, j, blk):
        t = src[e, base - SUBLANES:base + rows + SUBLANES, :].reshape(n_tiles + 2, SUBLANES, FF_TILE)
        down = pltpu.roll(t, 1, 1)
        up = pltpu.roll(t, SUBLANES - 1, 1)
        if L >= rows:
            starts_seq = (blk * rows) & (L - 1) == 0
            ends_seq = ((blk + 1) * rows) & (L - 1) == 0
            above = [jnp.where(starts_seq, zero_tile, down[0:1]), down[1:n_tiles]]
            below = [up[2:n_tiles + 1], jnp.where(ends_seq, zero_tile, up[n_tiles + 1:n_tiles + 2])]
        else:
            above, below = [], []
            for s in range(0, n_tiles, seq_tiles):
                above += [down[0:1] if s == 0 else zero_tile, down[s + 1:s + seq_tiles]]
                last_seq = s + seq_tiles == n_tiles
                below += [up[s + 2:s + seq_tiles + 1], up[n_tiles + 1:n_tiles + 2] if last_seq else zero_tile]
        above = jnp.where(sub == 0, jnp.concatenate(above, axis=0), down[1:n_tiles + 1])
        below = jnp.where(sub == SUBLANES - 1, jnp.concatenate(below, axis=0), up[1:n_tiles + 1])
        w = cw_ref[e, j]
        return above * w[0:1] + t[1:n_tiles + 1] * w[1:2] + below * w[2:3] + cb_ref[e, j]

    def down_proj(j, src, blk):
        gate = conv(src, 0, j, blk)
        act = gate / (1.0 + jnp.exp(-gate)) * conv(src, 1, j, blk)
        return _dot(act.reshape(rows, FF_TILE).astype(BF16), wd_ref[j])

    def start_block(blk):
        if halo:
            x_ref, a_ref, hy_ref, xp_ref, ap_ref, hp_ref, xn_ref, an_ref, hn_ref = rows_refs
            cat = lambda p, m, n: jnp.concatenate([p[...], m[...], n[...]], axis=0)
            x, a, hy = cat(xp_ref, x_ref, xn_ref), cat(ap_ref, a_ref, an_ref), cat(hp_ref, hy_ref, hn_ref)
        else:
            x, a, hy = (ref[...] for ref in rows_refs)
        gg = gg_ref[...]
        merged = jnp.concatenate([_rms(a) * gg[:, :ATTN_W], _rms(hy) * gg[:, ATTN_W:]], axis=-1).astype(BF16)
        x1 = x + mod(2, blk) * _dot(merged, wo_ref[...])
        h2_s[...] = (_rms(x1) * n2_ref[...] * (1.0 + mod(4, blk)) + mod(3, blk)).astype(BF16)
        x1_s[...] = x1[halo:halo + rows]
        up_proj(0, mc_s)
        up_proj(1, mb_s)
        acc_s[...] = down_proj(0, mc_s, blk)
        up_proj(2, ma_s)
        acc_s[...] += down_proj(1, mb_s, blk)

    def finish_block(blk):
        x2 = x1_s[...] + mod(5, blk) * (acc_s[...] + down_proj(last_tile, ma_s, blk))
        o_ref[...] = _rms(x2) * fg_ref[...]

    @pl.when(i == 0)
    def _():
        for buf in (ma_s, mb_s, mc_s):
            for e in range(2):
                buf[e, 0:TAIL_PAD, :] = jnp.zeros((TAIL_PAD, FF_TILE), F32)
                buf[e, TAIL_PAD + ext:, :] = jnp.zeros((TAIL_PAD, FF_TILE), F32)
        start_block(i)

    @pl.when(jnp.logical_and(i > 0, i < n_blocks))
    def _():
        finish_block(i - 1)
        start_block(i)

    @pl.when(i == n_blocks)
    def _():
        finish_block(i - 1)

    @pl.when(i < n_blocks)
    def _():
        def two_tiles(k, carry):
            j = 2 * k
            up_proj(j + 1, mb_s)
            acc_s[...] += down_proj(j, ma_s, i)
            up_proj(j + 2, ma_s)
            acc_s[...] += down_proj(j + 1, mb_s, i)
            return carry

        lax.fori_loop(1, (N_FF_TILES - 1) // 2, two_tiles, 0)


def _tail(x, a, hyo, mods, grp_g, w_out, n2_g, final_g, w_up, conv_w, conv_b, w_down, *, L, mod_base,
          blocks_per_mod):
    n = x.shape[0]
    n_blocks = n // TAIL_ROWS
    halo = TAIL_HALO if L > TAIL_ROWS else 0
    cur = lambda i: jnp.minimum(i, n_blocks - 1)
    row = lambda w: pl.BlockSpec((TAIL_ROWS, w), lambda i: (cur(i), 0))
    per_halo = TAIL_ROWS // TAIL_HALO
    prev = lambda w: pl.BlockSpec((TAIL_HALO, w), lambda i: (jnp.maximum(cur(i) * per_halo - 1, 0), 0))
    nxt = lambda w: pl.BlockSpec(
        (TAIL_HALO, w), lambda i: (jnp.minimum((cur(i) + 1) * per_halo, n // TAIL_HALO - 1), 0))
    once = lambda shape: pl.BlockSpec(shape, lambda i: (0,) * len(shape), pipeline_mode=pl.Buffered(1))
    widths = (D_MODEL, ATTN_W, HY_CH)
    row_specs = [row(w) for w in widths]
    row_args = [x, a, hyo]
    if halo:
        row_specs += [prev(w) for w in widths] + [nxt(w) for w in widths]
        row_args += [x, a, hyo, x, a, hyo]
    ext = TAIL_ROWS + 2 * halo
    staging = pltpu.VMEM((2, ext + 2 * TAIL_PAD, FF_TILE), F32)
    return pl.pallas_call(
        functools.partial(_tail_kernel, L=L, mod_base=mod_base, blocks_per_mod=blocks_per_mod, halo=halo,
                          n_blocks=n_blocks),
        grid=(n_blocks + 1,),
        in_specs=row_specs + [
            once((6, 8, D_MODEL)), once((1, D_MODEL)), once((D_MODEL, D_MODEL)), once((1, D_MODEL)),
            once((1, D_MODEL)),
            once((D_MODEL, 2 * D_FF)),
            once((2, N_FF_TILES, 3, FF_TILE)), once((2, N_FF_TILES, 1, FF_TILE)),
            once((N_FF_TILES, FF_TILE, D_MODEL))],
        out_specs=pl.BlockSpec((TAIL_ROWS, D_MODEL), lambda i: (jnp.maximum(i - 1, 0), 0)),
        out_shape=jax.ShapeDtypeStruct((n, D_MODEL), F32),
        scratch_shapes=[pltpu.VMEM((ext, D_MODEL), BF16), pltpu.VMEM((TAIL_ROWS, D_MODEL), F32),
                        pltpu.VMEM((TAIL_ROWS, D_MODEL), F32), staging, staging, staging],
        compiler_params=_cparams(("arbitrary",)),
        name=f"tail_{L}",
    )(*row_args, mods, grp_g, w_out, n2_g, final_g, w_up, conv_w, conv_b, w_down)


def kernel(x_prompt, x_sample, cache_ctx_k, cache_ctx_v, c, c_ctx, w_ada, b_ada, norm1_g, w_in, rpb,
           hy_conv_w, hy_conv_b, filt_w1, filt_b1, filt_w2, filt_b2, filt_w3, filt_b3, filt_freq,
           filt_bias, grp_norm_g, w_out, norm2_g, w_up, ffn_conv_w, ffn_conv_b, w_down, final_g):
    assert w_ada.shape[0] == 1, "single-layer trunk"
    bc, lc, _ = x_prompt.shape
    bl, ll, _ = x_sample.shape
    past = cache_ctx_k.shape[3]

    cvec = jnp.concatenate([c_ctx[None], c, jnp.zeros((8 - 1 - bl, D_MODEL), F32)], axis=0)
    mods = _mods(cvec, w_ada[0], b_ada[0])

    conv_w_t = ffn_conv_w[0].reshape(3, 2, N_FF_TILES, FF_TILE).transpose(1, 2, 0, 3)
    conv_b_t = ffn_conv_b[0].reshape(2, N_FF_TILES, 1, FF_TILE)
    g1 = norm1_g[0][None]
    filt = (filt_w1[0], filt_b1[0], filt_w2[0], filt_b2[0], filt_w3[0], filt_b3[0], filt_freq[0])

    def tables(L):
        fwd, inv = _dft_tables(L)
        return jnp.asarray(fwd).astype(BF16), jnp.asarray(inv).astype(BF16)

    xc = x_prompt.reshape(bc * lc, D_MODEL)
    a_fwd, a_inv = tables(lc)
    kf, w_in_b = _filter_spectrum(lc, a_fwd, *filt, casts=(w_in[0],))
    qkv, hy, state_k, state_v = _proj(xc, mods, g1, w_in_b, mod_base=0, blocks_per_mod=bc * lc // PROJ_ROWS,
                                      L=lc, emit_state=True)
    att, w_up_b = _ctx_attn(qkv, lc, casts=(w_up[0],))
    att, xs = lax.optimization_barrier((att, x_sample.reshape(bl * ll, D_MODEL)))
    hyo, w_down_b, w_out_b = _hyena(hy, hy_conv_w[0], hy_conv_b[0][None], kf, filt_bias[0], a_fwd, a_inv, lc,
                                    casts=(w_down[0], w_out[0]))
    shared_tail = (grp_norm_g[0][None], w_out_b, norm2_g[0][None], final_g[None], w_up_b, conv_w_t, conv_b_t,
                   w_down_b.reshape(N_FF_TILES, FF_TILE, D_MODEL))
    y_prompt = _tail(xc, att, hyo, mods, *shared_tail, L=lc, mod_base=0,
                     blocks_per_mod=bc * lc // TAIL_ROWS).reshape(bc, lc, D_MODEL)

    a_fwd, a_inv = tables(ll)
    kf, = _filter_spectrum(ll, a_fwd, *filt)
    qkv, hy = _proj(xs, mods, g1, w_in_b, mod_base=1, blocks_per_mod=ll // PROJ_ROWS, L=ll, emit_state=False)
    lanes_major = lambda t: t[:, 0].transpose(0, 2, 1, 3).reshape(bl, past, ATTN_W).astype(BF16)
    hyo, = _hyena(hy, hy_conv_w[0], hy_conv_b[0][None], kf, filt_bias[0], a_fwd, a_inv, ll)
    hyo, qkv = lax.optimization_barrier((hyo, qkv))
    att = _lat_attn(qkv, lanes_major(cache_ctx_k), lanes_major(cache_ctx_v), rpb[0], bl, ll)
    y_sample = _tail(xs, att, hyo, mods, *shared_tail, L=ll, mod_base=1,
                     blocks_per_mod=ll // TAIL_ROWS).reshape(bl, ll, D_MODEL)

    return (y_prompt, y_sample, jnp.swapaxes(state_k, 3, 4), jnp.swapaxes(state_v, 3, 4))
```
